```python
import math
import jax
import jax.numpy as jnp
from jax import lax
import numpy as np

D_MODEL = 1024
BATCH = 2
SEQ = 16384
DEPTH = 2

MEM_LEN = 256
GROUP_WIDTH = 384
MIX_WIDTH = 4 * GROUP_WIDTH
SSD_HEAD_DIM = 64
SSD_HEADS = GROUP_WIDTH // SSD_HEAD_DIM
SSD_GROUPS = 2
SSD_STATE = 128
SSD_CONV = 4
SSD_CHUNK = 128
SSD_CONV_CH = GROUP_WIDTH + 2 * SSD_GROUPS * SSD_STATE
DT_MIN = 0.001
DT_MAX = 0.1
SC_WIDTH = 3
ATT_HEAD_DIM = 64
ATT_HEADS = 6
ATT_WIDTH = ATT_HEADS * ATT_HEAD_DIM
DILATED_PATTERNS = ((128, 1), (512, 4), (2048, 16))
CF_KERNEL = 31
IN_SIZES = (GROUP_WIDTH, SSD_CONV_CH, SSD_HEADS,
            GROUP_WIDTH, GROUP_WIDTH, GROUP_WIDTH,
            ATT_WIDTH, ATT_WIDTH, ATT_WIDTH,
            GROUP_WIDTH, GROUP_WIDTH)
N_IN = 6 * GROUP_WIDTH + SSD_CONV_CH + SSD_HEADS + 3 * ATT_WIDTH
CA_HEADS = 4
CA_HEAD_DIM = D_MODEL // CA_HEADS
N_EXPERT_GROUPS = 4
EXPERTS_PER_GROUP = 8
N_EXPERTS = N_EXPERT_GROUPS * EXPERTS_PER_GROUP
TOP_K = 2
EXPERT_HIDDEN = D_MODEL // 2
MOE_BLOCK = 256

RMS_EPS = 1e-6
LN_EPS = 1e-5

kernel_name = 'hymba_style_ssd_conv_dilated_attn_hmoe'


def rms_norm(x, g):
    xf = x.astype(jnp.float32)
    y = xf * lax.rsqrt(jnp.mean(xf * xf, axis=-1, keepdims=True) + RMS_EPS)
    return (y * g.astype(jnp.float32)).astype(x.dtype)


def layer_norm(x, g, b):
    xf = x.astype(jnp.float32)
    mu = jnp.mean(xf, axis=-1, keepdims=True)
    xc = xf - mu
    y = xc * lax.rsqrt(jnp.mean(xc * xc, axis=-1, keepdims=True) + LN_EPS)
    return (y * g.astype(jnp.float32) + b.astype(jnp.float32)).astype(x.dtype)


def split_last(t, sizes):
    out, o = [], 0
    for s in sizes:
        out.append(t[..., o:o + s])
        o += s
    return out


def causal_dwconv(x, w, b=None):
    k, c = w.shape
    y = lax.conv_general_dilated(x, w[:, None, :].astype(x.dtype), window_strides=(1,),
                                 padding=((k - 1, 0),), dimension_numbers=('NWC', 'WIO', 'NWC'),
                                 feature_group_count=c)
    if b is not None:
        y = y + b.astype(x.dtype)
    return y


def ssd_chunked(x, da, b, c):
    bsz, s, nh, p = x.shape
    g, n = b.shape[2], b.shape[3]
    j = nh // g
    q = SSD_CHUNK
    nc = s // q
    x = x.reshape(bsz, nc, q, g, j, p)
    b = b.reshape(bsz, nc, q, g, n)
    c = c.reshape(bsz, nc, q, g, n)
    a_cs = jnp.cumsum(da.reshape(bsz, nc, q, g, j).transpose(0, 3, 4, 1, 2), axis=-1)
    tril = jnp.tril(jnp.ones((q, q), dtype=bool))
    seg = jnp.exp(jnp.where(tril, a_cs[..., :, None] - a_cs[..., None, :], -jnp.inf))
    cb = jnp.einsum('bclgn,bcsgn->bcgls', c, b)
    y_diag = jnp.einsum('bcgls,bgjcls,bcsgjp->bclgjp', cb, seg, x)
    decay = jnp.exp(a_cs[..., -1:] - a_cs)
    states = jnp.einsum('bclgn,bgjcl,bclgjp->bcgjpn', b, decay, x)
    chunk_decay = jnp.exp(a_cs[..., -1])

    def step(h, inp):
        st, dec = inp
        return h * dec[..., None, None] + st, h

    _, prev = lax.scan(step, jnp.zeros_like(states[:, 0]),
                       (jnp.moveaxis(states, 1, 0), jnp.moveaxis(chunk_decay, -1, 0)))
    prev = jnp.moveaxis(prev, 0, 1)
    y_off = jnp.einsum('bclgn,bcgjpn,bgjcl->bclgjp', c, prev, jnp.exp(a_cs))
    return (y_diag + y_off).reshape(bsz, s, nh, p)


def dilated_branch(q, k, v, window, dilation):
    bsz, s, nh, dh = q.shape
    n = window // dilation
    blk = n
    span = dilation * blk
    sp = -(-s // span) * span
    nb = sp // span

    def to_blocks(t):
        t = jnp.pad(t.astype(jnp.float32), ((0, 0), (0, sp - s), (0, 0), (0, 0)))
        return t.reshape(bsz, nb, blk, dilation, nh, dh).transpose(0, 3, 4, 1, 2, 5)

    def band(t):
        prev = jnp.pad(t, ((0, 0), (0, 0), (0, 0), (1, 0), (0, 0), (0, 0)))[:, :, :, :-1]
        return jnp.concatenate([prev, t], axis=4)

    qb = to_blocks(q)
    kb = band(to_blocks(k))
    vb = band(to_blocks(v))
    sc = jnp.einsum('brhnqe,brhnke->brhnqk', qb, kb) * (ATT_HEAD_DIM ** -0.5)
    bi = jnp.arange(nb)[:, None, None]
    qq = jnp.arange(blk)[None, :, None]
    kk = jnp.arange(2 * blk)[None, None, :]
    rel = qq + blk - kk
    valid = (rel >= 0) & (rel <= n) & ((bi > 0) | (kk >= blk))
    sc = jnp.where(valid, sc, -jnp.inf)
    m = jnp.max(sc, axis=-1, keepdims=True)
    pr = jnp.exp(sc - m)
    den = jnp.sum(pr, axis=-1)
    o = jnp.einsum('brhnqk,brhnke->brhnqe', pr, vb) / den[..., None]
    lse = m[..., 0] + jnp.log(den)
    o = o.transpose(0, 3, 4, 1, 2, 5).reshape(bsz, sp, nh, dh)[:, :s]
    lse = lse.transpose(0, 3, 4, 1, 2).reshape(bsz, sp, nh)[:, :s]
    return o, lse


def dilated_attention(q, k, v, q_norm, k_norm):
    bsz, s, _ = q.shape
    q = rms_norm(q.reshape(bsz, s, ATT_HEADS, ATT_HEAD_DIM), q_norm)
    k = rms_norm(k.reshape(bsz, s, ATT_HEADS, ATT_HEAD_DIM), k_norm)
    v = v.reshape(bsz, s, ATT_HEADS, ATT_HEAD_DIM)
    outs, lses = [], []
    for window, dilation in DILATED_PATTERNS:
        o, l = dilated_branch(q, k, v, window, dilation)
        outs.append(o)
        lses.append(l)
    wts = jax.nn.softmax(jnp.stack(lses, axis=0), axis=0)
    o = jnp.sum(wts[..., None] * jnp.stack(outs, axis=0), axis=0)
    return o.reshape(bsz, s, ATT_WIDTH).astype(v.dtype)


def hybrid_mixer(h, w_in, ssd_conv_w, ssd_conv_b, ssd_dt_bias, ssd_a_log, ssd_d, ssd_norm,
                 sc_conv_w, attn_q_norm, attn_k_norm, cf_conv_w, cf_conv_b, cf_ln_g, cf_ln_b, w_out):
    bsz, s, _ = h.shape
    proj = h @ w_in
    (z, xbc, dt_raw, sc_b, sc_c, sc_h, q, k, v, cf_a, cf_g) = split_last(proj, IN_SIZES)

    xbc = jax.nn.silu(causal_dwconv(xbc, ssd_conv_w, ssd_conv_b))
    xs, bm, cm = split_last(xbc, (GROUP_WIDTH, SSD_GROUPS * SSD_STATE, SSD_GROUPS * SSD_STATE))
    xs = xs.reshape(bsz, s, SSD_HEADS, SSD_HEAD_DIM).astype(jnp.float32)
    bm = bm.reshape(bsz, s, SSD_GROUPS, SSD_STATE).astype(jnp.float32)
    cm = cm.reshape(bsz, s, SSD_GROUPS, SSD_STATE).astype(jnp.float32)
    dt = jax.nn.softplus(dt_raw.astype(jnp.float32) + ssd_dt_bias.astype(jnp.float32))
    a = -jnp.exp(ssd_a_log.astype(jnp.float32))
    y = ssd_chunked(xs * dt[..., None], dt * a, bm, cm) + xs * ssd_d.astype(jnp.float32)[:, None]
    y = y.reshape(bsz, s, GROUP_WIDTH)
    y_ssd = rms_norm(y * jax.nn.silu(z.astype(jnp.float32)), ssd_norm).astype(h.dtype)

    y_sc = sc_b * causal_dwconv(sc_c * sc_h, sc_conv_w)

    y_att = dilated_attention(q, k, v, attn_q_norm, attn_k_norm)

    u = causal_dwconv(cf_a * jax.nn.sigmoid(cf_g), cf_conv_w, cf_conv_b)
    y_cf = jax.nn.silu(layer_norm(u, cf_ln_g, cf_ln_b))

    return jnp.concatenate([y_ssd, y_sc, y_att, y_cf], axis=-1) @ w_out


def memory_cross_attention(h, mem_n, wq, wkv, q_norm, k_norm, wo):
    bsz, s, _ = h.shape
    m = mem_n.shape[1]
    q = rms_norm((h @ wq).reshape(bsz, s, CA_HEADS, CA_HEAD_DIM), q_norm)
    k, v = split_last(mem_n @ wkv, (D_MODEL, D_MODEL))
    k = rms_norm(k.reshape(bsz, m, CA_HEADS, CA_HEAD_DIM), k_norm)
    v = v.reshape(bsz, m, CA_HEADS, CA_HEAD_DIM)
    sc = jnp.einsum('bshe,bmhe->bhsm', q.astype(jnp.float32), k.astype(jnp.float32)) * (CA_HEAD_DIM ** -0.5)
    pr = jax.nn.softmax(sc, axis=-1)
    o = jnp.einsum('bhsm,bmhe->bshe', pr, v.astype(jnp.float32)).astype(h.dtype)
    return o.reshape(bsz, s, D_MODEL) @ wo


def hierarchical_moe(h, wg, bg, we, be, w_gate, w_up, w_down):
    bsz, s, d = h.shape
    t = bsz * s
    ht = h.reshape(t, d)
    g_prob = jax.nn.softmax((ht @ wg + bg).astype(jnp.float32), axis=-1)
    g_p, g_idx = lax.top_k(g_prob, 1)
    e_logits = (ht @ we + be).astype(jnp.float32).reshape(t, N_EXPERT_GROUPS, EXPERTS_PER_GROUP)
    e_in = e_logits[jnp.arange(t), g_idx[:, 0]]
    e_p, e_loc = lax.top_k(jax.nn.softmax(e_in, axis=-1), TOP_K)
    gates = g_p * e_p / jnp.sum(e_p, axis=-1, keepdims=True)
    expert = g_idx * EXPERTS_PER_GROUP + e_loc

    a = t * TOP_K
    flat_e = expert.reshape(a)
    flat_tok = jnp.repeat(jnp.arange(t), TOP_K)
    flat_g = gates.reshape(a)
    order = jnp.argsort(flat_e)
    e_s, tok_s, g_s = flat_e[order], flat_tok[order], flat_g[order]
    counts = jnp.zeros((N_EXPERTS,), jnp.int32).at[flat_e].add(1)
    padded = ((counts + MOE_BLOCK - 1) // MOE_BLOCK) * MOE_BLOCK
    ends_pad = jnp.cumsum(padded)
    starts_pad = ends_pad - padded
    starts = jnp.cumsum(counts) - counts
    dest = starts_pad[e_s] + jnp.arange(a) - starts[e_s]
    p_rows = (-(-a // MOE_BLOCK)) * MOE_BLOCK + N_EXPERTS * MOE_BLOCK
    n_blocks = p_rows // MOE_BLOCK
    buf = jnp.zeros((p_rows, d), h.dtype).at[dest].set(ht[tok_s])
    blk_e = jnp.minimum(jnp.searchsorted(ends_pad, jnp.arange(n_blocks) * MOE_BLOCK, side='right'),
                        N_EXPERTS - 1)

    def expert_block(args):
        xb, e = args
        hid = jax.nn.silu(xb @ w_gate[e]) * (xb @ w_up[e])
        return hid @ w_down[e]

    yb = lax.map(expert_block, (buf.reshape(n_blocks, MOE_BLOCK, d), blk_e)).reshape(p_rows, d)
    y = jnp.zeros((t, d), h.dtype).at[tok_s].add(yb[dest] * g_s[:, None].astype(h.dtype))
    return y.reshape(bsz, s, d)


def setup_inputs(seed: int = 0) -> dict:
    key = jax.random.key(seed)
    keys = iter(jax.random.split(key, 48))
    f32 = jnp.float32
    L = DEPTH

    def normal(shape, scale):
        return jax.random.normal(next(keys), shape, f32) * scale

    def gain(shape):
        return 1.0 + normal(shape, 0.02)

    x = normal((BATCH, SEQ, D_MODEL), 1.0)
    mem = normal((BATCH, MEM_LEN, D_MODEL), 1.0)
    dt_init = jnp.exp(jax.random.uniform(next(keys), (L, SSD_HEADS), f32,
                                         math.log(DT_MIN), math.log(DT_MAX)))
    ssd_dt_bias = dt_init + jnp.log(-jnp.expm1(-dt_init))
    ssd_a_log = jnp.log(jax.random.uniform(next(keys), (L, SSD_HEADS), f32, 1.0, 16.0))
    return {
        'x': x,
        'mem': mem,
        'norm_mix': gain((L, D_MODEL)),
        'w_in': normal((L, D_MODEL, N_IN), D_MODEL ** -0.5),
        'ssd_conv_w': normal((L, SSD_CONV, SSD_CONV_CH), SSD_CONV ** -0.5),
        'ssd_conv_b': normal((L, SSD_CONV_CH), 0.02),
        'ssd_dt_bias': ssd_dt_bias,
        'ssd_a_log': ssd_a_log,
        'ssd_d': 1.0 + normal((L, SSD_HEADS), 0.1),
        'ssd_norm': gain((L, GROUP_WIDTH)),
        'sc_conv_w': normal((L, SC_WIDTH, GROUP_WIDTH), SC_WIDTH ** -0.5),
        'attn_q_norm': gain((L, ATT_HEAD_DIM)),
        'attn_k_norm': gain((L, ATT_HEAD_DIM)),
        'cf_conv_w': normal((L, CF_KERNEL, GROUP_WIDTH), CF_KERNEL ** -0.5),
        'cf_conv_b': normal((L, GROUP_WIDTH), 0.02),
        'cf_ln_g': gain((L, GROUP_WIDTH)),
        'cf_ln_b': normal((L, GROUP_WIDTH), 0.02),
        'w_out': normal((L, MIX_WIDTH, D_MODEL), MIX_WIDTH ** -0.5),
        'norm_ca': gain((L, D_MODEL)),
        'norm_mem': gain((L, D_MODEL)),
        'ca_wq': normal((L, D_MODEL, D_MODEL), D_MODEL ** -0.5),
        'ca_wkv': normal((L, D_MODEL, 2 * D_MODEL), D_MODEL ** -0.5),
        'ca_q_norm': gain((L, CA_HEAD_DIM)),
        'ca_k_norm': gain((L, CA_HEAD_DIM)),
        'ca_wo': normal((L, D_MODEL, D_MODEL), D_MODEL ** -0.5),
        'norm_ffn': gain((L, D_MODEL)),
        'router_group_w': normal((L, D_MODEL, N_EXPERT_GROUPS), D_MODEL ** -0.5),
        'router_group_b': normal((L, N_EXPERT_GROUPS), 0.01),
        'router_expert_w': normal((L, D_MODEL, N_EXPERTS), D_MODEL ** -0.5),
        'router_expert_b': normal((L, N_EXPERTS), 0.01),
        'exp_w_gate': normal((L, N_EXPERTS, D_MODEL, EXPERT_HIDDEN), D_MODEL ** -0.5),
        'exp_w_up': normal((L, N_EXPERTS, D_MODEL, EXPERT_HIDDEN), D_MODEL ** -0.5),
        'exp_w_down': normal((L, N_EXPERTS, EXPERT_HIDDEN, D_MODEL), EXPERT_HIDDEN ** -0.5),
    }


def reference(x, mem, norm_mix, w_in, ssd_conv_w, ssd_conv_b, ssd_dt_bias, ssd_a_log, ssd_d, ssd_norm,
              sc_conv_w, attn_q_norm, attn_k_norm, cf_conv_w, cf_conv_b, cf_ln_g, cf_ln_b, w_out,
              norm_ca, norm_mem, ca_wq, ca_wkv, ca_q_norm, ca_k_norm, ca_wo,
              norm_ffn, router_group_w, router_group_b, router_expert_w, router_expert_b,
              exp_w_gate, exp_w_up, exp_w_down):
    for l in range(DEPTH):
        h = rms_norm(x, norm_mix[l])
        x = x + hybrid_mixer(h, w_in[l], ssd_conv_w[l], ssd_conv_b[l], ssd_dt_bias[l], ssd_a_log[l],
                             ssd_d[l], ssd_norm[l], sc_conv_w[l], attn_q_norm[l], attn_k_norm[l],
                             cf_conv_w[l], cf_conv_b[l], cf_ln_g[l], cf_ln_b[l], w_out[l])
        h = rms_norm(x, norm_ca[l])
        x = x + memory_cross_attention(h, rms_norm(mem, norm_mem[l]), ca_wq[l], ca_wkv[l],
                                       ca_q_norm[l], ca_k_norm[l], ca_wo[l])
        h = rms_norm(x, norm_ffn[l])
        x = x + hierarchical_moe(h, router_group_w[l], router_group_b[l], router_expert_w[l],
                                 router_expert_b[l], exp_w_gate[l], exp_w_up[l], exp_w_down[l])
    return x
```

```python
import functools

import numpy as np
import jax
import jax.numpy as jnp
from jax import lax
from jax.experimental import pallas as pl
from jax.experimental.pallas import tpu as pltpu

f32 = jnp.float32
bf16 = jnp.bfloat16

D_MODEL = 1024
GROUP_WIDTH = 384
SSD_HEADS = 6
SSD_HEAD_DIM = 64
SSD_GROUPS = 2
SSD_STATE = 128
SSD_CONV = 4
SSD_CHUNK = 128
SSD_CONV_CH = GROUP_WIDTH + 2 * SSD_GROUPS * SSD_STATE
SC_WIDTH = 3
ATT_HEADS = 6
ATT_HEAD_DIM = 64
ATT_BLOCK = 128
DILATIONS = (1, 4, 16)
ATT_TILE = ATT_BLOCK * max(DILATIONS)
CF_KERNEL = 31
CA_HEADS = 4
CA_HEAD_DIM = D_MODEL // CA_HEADS
N_EXPERT_GROUPS = 4
EXPERTS_PER_GROUP = 8
N_EXPERTS = N_EXPERT_GROUPS * EXPERTS_PER_GROUP
TOP_K = 2
EXPERT_HIDDEN = D_MODEL // 2
MOE_BLOCK = 256
RMS_EPS = 1e-6
LN_EPS = 1e-5

LANES = 128
SUBLANES = 8
VMEM_LIMIT_BYTES = 56 * 1024 * 1024

TM_PROJ = 512
TS_SSD = 1024
TS_CONV = 512
CONV_ROWS = 32
TM_CA = 512
TM_ROUTER = 512
TD_DISPATCH = 512
TC_COMBINE = 512

_SEGS = (("xbc", SSD_CONV_CH), ("dt", LANES), ("z", GROUP_WIDTH), ("sc_b", GROUP_WIDTH), ("sc_c", GROUP_WIDTH),
         ("sc_h", GROUP_WIDTH), ("q", GROUP_WIDTH), ("k", GROUP_WIDTH), ("v", GROUP_WIDTH), ("cf_a", GROUP_WIDTH),
         ("cf_g", GROUP_WIDTH))


def _params(n_axes):
    return pltpu.CompilerParams(dimension_semantics=("arbitrary",) * n_axes, vmem_limit_bytes=VMEM_LIMIT_BYTES)


def _dot(a, b):
    return jnp.dot(a, b, preferred_element_type=f32)


def _dot_nt(a, b):
    return lax.dot_general(a, b, (((1,), (1,)), ((), ())), preferred_element_type=f32)


def _split2(x):
    hi = x.astype(bf16)
    lo = (x - hi.astype(f32)).astype(bf16)
    return hi, lo


def _split3(x):
    hi = x.astype(bf16)
    r = x - hi.astype(f32)
    mid = r.astype(bf16)
    lo = (r - mid.astype(f32)).astype(bf16)
    return hi, mid, lo


def _dot_exact_rhs(x, m_bf16, n_split):
    parts = _split3(x) if n_split == 3 else _split2(x)
    acc = _dot(parts[0], m_bf16)
    for p in parts[1:]:
        acc = acc + _dot(p, m_bf16)
    return acc


def _sigmoid(x):
    return 1.0 / (1.0 + jnp.exp(-x))


def _silu(x):
    return x * _sigmoid(x)


def _rms(x, gain):
    ms = jnp.mean(x * x, axis=-1, keepdims=True)
    return x * lax.rsqrt(ms + RMS_EPS) * gain


def _inproj_kernel(x_ref, g_ref, *refs):
    n = len(_SEGS)
    w_refs, o_refs = refs[:n], refs[n:]
    h = _rms(x_ref[...], g_ref[...]).astype(bf16)
    for w_ref, o_ref in zip(w_refs, o_refs):
        o_ref[...] = _dot(h, w_ref[...])


def _in_projection(x2d, gain, w_segs):
    t = x2d.shape[0]
    tm = TM_PROJ
    in_specs = [pl.BlockSpec((tm, D_MODEL), lambda i: (i, 0)), pl.BlockSpec((1, D_MODEL), lambda i: (0, 0))]
    in_specs += [pl.BlockSpec((D_MODEL, w), lambda i: (0, 0)) for _, w in _SEGS]
    out_specs = [pl.BlockSpec((tm, w), lambda i: (i, 0)) for _, w in _SEGS]
    out_shape = [jax.ShapeDtypeStruct((t, w), f32) for _, w in _SEGS]
    outs = pl.pallas_call(
        _inproj_kernel, grid=(t // tm,), in_specs=in_specs, out_specs=out_specs, out_shape=out_shape,
        compiler_params=_params(1), name="in_projection",
    )(x2d, gain, *w_segs)
    return dict(zip([n for n, _ in _SEGS], outs))


def _ssd_kernel(xbc_ref, dt_ref, z_ref, cw_ref, cb_ref, dtb_ref, alog_ref, dvec_ref, ng_ref, tril_ref, e384_ref,
                e768_ref, o_ref, xbuf, xc, state):
    ts = TS_SSD
    q = SSD_CHUNK
    halo = SUBLANES

    @pl.when(pl.program_id(1) == 0)
    def _():
        xbuf[0:halo, :] = jnp.zeros((halo, SSD_CONV_CH), f32)
        state[...] = jnp.zeros_like(state)

    xbuf[halo:halo + ts, :] = xbc_ref[...]
    acc = cb_ref[...] + cw_ref[0:1, :] * xbuf[halo - 3:halo - 3 + ts, :]
    for k in range(1, SSD_CONV):
        acc = acc + cw_ref[k:k + 1, :] * xbuf[halo - 3 + k:halo - 3 + k + ts, :]
    xc[...] = _silu(acc)
    xbuf[0:halo, :] = xbuf[ts:ts + halo, :]

    lane_row = lax.broadcasted_iota(jnp.int32, (1, LANES), 1)
    head_lane = lane_row < SSD_HEADS
    a_row = jnp.where(head_lane, -jnp.exp(alog_ref[...]), 0.0)
    lane_sq = lax.broadcasted_iota(jnp.int32, (q, LANES), 1)
    row_sq = lax.broadcasted_iota(jnp.int32, (q, LANES), 0)
    causal = row_sq >= lane_sq
    left = lane_sq < SSD_HEAD_DIM
    tril = tril_ref[...]
    e384 = e384_ref[...]
    e768 = e768_ref[...]
    dvec = dvec_ref[...]
    ng = ng_ref[...]

    def chunk(c, carry):
        r0 = pl.multiple_of(c * q, q)
        dtr = dt_ref[pl.ds(r0, q), :] + dtb_ref[...]
        sp = jnp.maximum(dtr, 0.0) + jnp.log1p(jnp.exp(-jnp.abs(dtr)))
        dt = jnp.where(head_lane, sp, 0.0)
        da = dt * a_row
        d3 = _split3(da)
        cs = _dot(tril, d3[0]) + _dot(tril, d3[1]) + _dot(tril, d3[2])
        cs_t = cs.T
        c3 = _split3(cs)
        cs384 = _dot(c3[0], e384) + _dot(c3[1], e384) + _dot(c3[2], e384)
        cs768 = _dot(c3[0], e768) + _dot(c3[1], e768) + _dot(c3[2], e768)
        dt384 = _dot_exact_rhs(dt, e384, 2)
        last384 = cs384[q - 1:q, :]
        decay384 = jnp.exp(last384 - cs384)
        expcs384 = jnp.exp(cs384)

        xs = xc[pl.ds(r0, q), 0:GROUP_WIDTH]
        bm = xc[pl.ds(r0, q), GROUP_WIDTH:GROUP_WIDTH + SSD_GROUPS * SSD_STATE]
        cm = xc[pl.ds(r0, q), GROUP_WIDTH + SSD_GROUPS * SSD_STATE:SSD_CONV_CH]
        xdt = xs * dt384
        xdt_b = xdt.astype(bf16)
        xdec_b = (xdt * decay384).astype(bf16)
        st = state[...]
        st_b = st.astype(bf16)

        cb_g, bt_g, c_g = [], [], []
        for g in range(SSD_GROUPS):
            b_f = bm[:, g * SSD_STATE:(g + 1) * SSD_STATE]
            c_b = cm[:, g * SSD_STATE:(g + 1) * SSD_STATE].astype(bf16)
            cb_g.append(_dot_nt(c_b, b_f.astype(bf16)))
            bt_g.append(b_f.T.astype(bf16))
            c_g.append(c_b)

        y_parts, st_parts = [], []
        for p in range(SSD_HEADS // 2):
            sl = slice(p * LANES, (p + 1) * LANES)
            xdt_p, xdec_p, st_p = xdt_b[:, sl], xdec_b[:, sl], st_b[:, sl]
            yd, s_new, yo = [], {}, {}
            for hh in range(2):
                h = 2 * p + hh
                g = h // (SSD_HEADS // SSD_GROUPS)
                diff = cs768[:, h * LANES:(h + 1) * LANES] - cs_t[h:h + 1, :]
                seg = jnp.exp(jnp.where(causal, diff, -1e30))
                m = (cb_g[g] * seg).astype(bf16)
                yd.append(_dot(m, xdt_p))
                if g not in s_new:
                    s_new[g] = _dot(bt_g[g], xdec_p)
                    yo[g] = _dot(c_g[g], st_p)
            g0 = (2 * p) // (SSD_HEADS // SSD_GROUPS)
            g1 = (2 * p + 1) // (SSD_HEADS // SSD_GROUPS)
            y_diag = jnp.where(left, yd[0], yd[1])
            y_off = jnp.where(left, yo[g0], yo[g1])
            s_pair = jnp.where(left, s_new[g0], s_new[g1])
            y_parts.append(y_diag + y_off * expcs384[:, sl] + xs[:, sl] * dvec[:, sl])
            st_parts.append(st[:, sl] * expcs384[q - 1:q, sl] + s_pair)
        state[...] = jnp.concatenate(st_parts, axis=-1)
        y = jnp.concatenate(y_parts, axis=-1)
        yg = y * _silu(z_ref[pl.ds(r0, q), :])
        o_ref[pl.ds(r0, q), :] = _rms(yg, ng)
        return carry

    lax.fori_loop(0, ts // q, chunk, 0)


def _ssd_constants():
    q = SSD_CHUNK
    tril = np.tril(np.ones((q, q), np.float32))
    e384 = np.zeros((LANES, GROUP_WIDTH), np.float32)
    e768 = np.zeros((LANES, SSD_HEADS * LANES), np.float32)
    for h in range(SSD_HEADS):
        e384[h, h * SSD_HEAD_DIM:(h + 1) * SSD_HEAD_DIM] = 1.0
        e768[h, h * LANES:(h + 1) * LANES] = 1.0
    return jnp.asarray(tril, bf16), jnp.asarray(e384, bf16), jnp.asarray(e768, bf16)


def _ssd(proj, bsz, seq, conv_w, conv_b, dt_bias, a_log, d_skip, norm_g):
    t = bsz * seq
    ts = TS_SSD
    ns = seq // ts
    tril, e384, e768 = _ssd_constants()
    pad = LANES - SSD_HEADS
    dtb = jnp.pad(dt_bias, (0, pad)).reshape(1, LANES)
    alog = jnp.pad(a_log, (0, pad)).reshape(1, LANES)
    dvec = jnp.repeat(d_skip, SSD_HEAD_DIM).reshape(1, GROUP_WIDTH)
    row = lambda b, i: (b * ns + i, 0)
    const = lambda b, i: (0, 0)
    return pl.pallas_call(
        _ssd_kernel, grid=(bsz, ns),
        in_specs=[pl.BlockSpec((ts, SSD_CONV_CH), row), pl.BlockSpec((ts, LANES), row), pl.BlockSpec((ts, GROUP_WIDTH), row),
                  pl.BlockSpec((SSD_CONV, SSD_CONV_CH), const), pl.BlockSpec((1, SSD_CONV_CH), const),
                  pl.BlockSpec((1, LANES), const), pl.BlockSpec((1, LANES), const), pl.BlockSpec((1, GROUP_WIDTH), const),
                  pl.BlockSpec((1, GROUP_WIDTH), const), pl.BlockSpec((SSD_CHUNK, SSD_CHUNK), const),
                  pl.BlockSpec((LANES, GROUP_WIDTH), const), pl.BlockSpec((LANES, SSD_HEADS * LANES), const)],
        out_specs=pl.BlockSpec((ts, GROUP_WIDTH), row),
        out_shape=jax.ShapeDtypeStruct((t, GROUP_WIDTH), f32),
        scratch_shapes=[pltpu.VMEM((ts + SUBLANES, SSD_CONV_CH), f32), pltpu.VMEM((ts, SSD_CONV_CH), f32),
                        pltpu.VMEM((SSD_STATE, GROUP_WIDTH), f32)],
        compiler_params=_params(2), name="ssd_scan",
    )(proj["xbc"], proj["dt"], proj["z"], conv_w, conv_b.reshape(1, -1), dtb, alog, dvec, norm_g.reshape(1, -1), tril,
      e384, e768)


_CF_HALO = 32


def _conv_kernel(scb_ref, scc_ref, sch_ref, cfa_ref, cfg_ref, scw_ref, cfw_ref, cfb_ref, lng_ref, lnb_ref, ysc_ref,
                 ycf_ref, ubuf, gbuf):
    ts = TS_CONV
    uh = SUBLANES

    @pl.when(pl.program_id(1) == 0)
    def _():
        ubuf[0:uh, :] = jnp.zeros((uh, GROUP_WIDTH), f32)
        gbuf[0:_CF_HALO, :] = jnp.zeros((_CF_HALO, GROUP_WIDTH), f32)

    ubuf[uh:uh + ts, :] = scc_ref[...] * sch_ref[...]
    gbuf[_CF_HALO:_CF_HALO + ts, :] = cfa_ref[...] * _sigmoid(cfg_ref[...])

    o = uh - (SC_WIDTH - 1)
    acc = scw_ref[0:1, :] * ubuf[o:o + ts, :]
    for k in range(1, SC_WIDTH):
        acc = acc + scw_ref[k:k + 1, :] * ubuf[o + k:o + k + ts, :]
    ysc_ref[...] = scb_ref[...] * acc

    o = _CF_HALO - (CF_KERNEL - 1)
    cfb = cfb_ref[...]
    lng = lng_ref[...]
    lnb = lnb_ref[...]
    for c in range(ts // CONV_ROWS):
        r0 = c * CONV_ROWS
        acc = cfb + cfw_ref[0:1, :] * gbuf[r0 + o:r0 + o + CONV_ROWS, :]
        for k in range(1, CF_KERNEL):
            acc = acc + cfw_ref[k:k + 1, :] * gbuf[r0 + o + k:r0 + o + k + CONV_ROWS, :]
        mu = jnp.mean(acc, axis=-1, keepdims=True)
        xc = acc - mu
        var = jnp.mean(xc * xc, axis=-1, keepdims=True)
        y = xc * lax.rsqrt(var + LN_EPS) * lng + lnb
        ycf_ref[r0:r0 + CONV_ROWS, :] = _silu(y)

    ubuf[0:uh, :] = ubuf[ts:ts + uh, :]
    gbuf[0:_CF_HALO, :] = gbuf[ts:ts + _CF_HALO, :]


def _convs(proj, bsz, seq, sc_w, cf_w, cf_b, ln_g, ln_b):
    t = bsz * seq
    ts = TS_CONV
    ns = seq // ts
    row = lambda b, i: (b * ns + i, 0)
    const = lambda b, i: (0, 0)
    act = pl.BlockSpec((ts, GROUP_WIDTH), row)
    vec = pl.BlockSpec((1, GROUP_WIDTH), const)
    return pl.pallas_call(
        _conv_kernel, grid=(bsz, ns),
        in_specs=[act, act, act, act, act, pl.BlockSpec((SC_WIDTH, GROUP_WIDTH), const),
                  pl.BlockSpec((CF_KERNEL, GROUP_WIDTH), const), vec, vec, vec],
        out_specs=[act, act],
        out_shape=[jax.ShapeDtypeStruct((t, GROUP_WIDTH), f32)] * 2,
        scratch_shapes=[pltpu.VMEM((ts + SUBLANES, GROUP_WIDTH), f32), pltpu.VMEM((ts + _CF_HALO, GROUP_WIDTH), f32)],
        compiler_params=_params(2), name="gated_convs",
    )(proj["sc_b"], proj["sc_c"], proj["sc_h"], proj["cf_a"], proj["cf_g"], sc_w, cf_w, cf_b.reshape(1, -1),
      ln_g.reshape(1, -1), ln_b.reshape(1, -1))


def _attn_kernel(q_ref, k_ref, v_ref, qg_ref, kg_ref, bsum_ref, o_ref, qbuf, kbuf, vbuf, acc_o, acc_m, acc_l):
    ta = ATT_TILE
    n = ATT_BLOCK
    tile = pl.program_id(2)

    @pl.when(tile == 0)
    def _():
        kbuf[0:ta, :] = jnp.zeros((ta, LANES), f32)
        vbuf[0:ta, :] = jnp.zeros((ta, LANES), f32)

    bsum = bsum_ref[...]

    def head_rms(x, g):
        ms = _dot_exact_rhs(x * x, bsum, 2)
        return x * lax.rsqrt(ms + RMS_EPS) * g

    qbuf[...] = head_rms(q_ref[...], qg_ref[...]) * (ATT_HEAD_DIM ** -0.5)
    kbuf[ta:2 * ta, :] = head_rms(k_ref[...], kg_ref[...])
    vbuf[ta:2 * ta, :] = v_ref[...]

    qq = lax.broadcasted_iota(jnp.int32, (n, 2 * n), 0)
    kk = lax.broadcasted_iota(jnp.int32, (n, 2 * n), 1)
    rel = qq + n - kk
    band = (rel >= 0) & (rel <= n)
    cur = kk >= n
    lane = lax.broadcasted_iota(jnp.int32, (n, LANES), 1)
    left = lane < ATT_HEAD_DIM

    for d in DILATIONS:
        shift = d.bit_length() - 1

        def rows(start, size, d=d):
            return pl.ds(start, size) if d == 1 else pl.ds(start, size, stride=d)

        def body(idx, carry, d=d, shift=shift, rows=rows):
            nb = idx >> shift
            r = idx & (d - 1)
            qs = nb * (n * d) + r
            ks = ta + qs - n * d
            q_r = qbuf[rows(qs, n), :]
            k_r = kbuf[rows(ks, 2 * n), :].astype(bf16)
            v_r = vbuf[rows(ks, 2 * n), :].astype(bf16)
            has_prev = (tile > 0) | (nb > 0)
            valid = band & (cur | has_prev)
            o_h, m_h, l_h = [], [], []
            for hh in range(2):
                qm = jnp.where(left if hh == 0 else ~left, q_r, 0.0).astype(bf16)
                s = jnp.where(valid, _dot_nt(qm, k_r), -1e30)
                m = jnp.max(s, axis=-1, keepdims=True)
                p = jnp.exp(s - m)
                l_h.append(jnp.sum(p, axis=-1, keepdims=True))
                m_h.append(m)
                o_h.append(_dot(p.astype(bf16), v_r))
            o_new = jnp.where(left, o_h[0], o_h[1])
            m_new = jnp.where(left, m_h[0], m_h[1])
            l_new = jnp.where(left, l_h[0], l_h[1])
            if d == 1:
                acc_o[rows(qs, n), :] = o_new
                acc_m[rows(qs, n), :] = m_new
                acc_l[rows(qs, n), :] = l_new
            else:
                o_old = acc_o[rows(qs, n), :]
                m_old = acc_m[rows(qs, n), :]
                l_old = acc_l[rows(qs, n), :]
                m_tot = jnp.maximum(m_old, m_new)
                a = jnp.exp(m_old - m_tot)
                b = jnp.exp(m_new - m_tot)
                acc_o[rows(qs, n), :] = a * o_old + b * o_new
                acc_l[rows(qs, n), :] = a * l_old + b * l_new
                acc_m[rows(qs, n), :] = m_tot
            return carry

        lax.fori_loop(0, ta // n, body, 0)

    o_ref[...] = acc_o[...] / acc_l[...]
    kbuf[0:ta, :] = kbuf[ta:2 * ta, :]
    vbuf[0:ta, :] = vbuf[ta:2 * ta, :]


def _dilated_attention(proj, bsz, seq, q_norm, k_norm):
    t = bsz * seq
    ta = ATT_TILE
    nt = seq // ta
    pairs = ATT_HEADS // 2
    bsum = np.zeros((LANES, LANES), np.float32)
    for h in range(2):
        bsum[h * ATT_HEAD_DIM:(h + 1) * ATT_HEAD_DIM, h * ATT_HEAD_DIM:(h + 1) * ATT_HEAD_DIM] = 1.0 / ATT_HEAD_DIM
    blk = pl.BlockSpec((ta, LANES), lambda b, p, i: (b * nt + i, p))
    const = lambda b, p, i: (0, 0)
    vec = pl.BlockSpec((1, LANES), const)
    buf = lambda rows: pltpu.VMEM((rows, LANES), f32)
    return pl.pallas_call(
        _attn_kernel, grid=(bsz, pairs, nt),
        in_specs=[blk, blk, blk, vec, vec, pl.BlockSpec((LANES, LANES), const)],
        out_specs=blk,
        out_shape=jax.ShapeDtypeStruct((t, GROUP_WIDTH), f32),
        scratch_shapes=[buf(ta), buf(2 * ta), buf(2 * ta), buf(ta), buf(ta), buf(ta)],
        compiler_params=_params(3), name="dilated_attention",
    )(proj["q"], proj["k"], proj["v"], jnp.tile(q_norm, 2).reshape(1, LANES), jnp.tile(k_norm, 2).reshape(1, LANES),
      jnp.asarray(bsum, bf16))


def _memkv_kernel(mem_ref, gm_ref, wkv_ref, gk_ref, k_ref, v_ref):
    hm = _rms(mem_ref[...], gm_ref[...]).astype(bf16)
    kv = _dot(hm, wkv_ref[...])
    gk = gk_ref[...]
    for hh in range(CA_HEADS):
        sl = slice(hh * CA_HEAD_DIM, (hh + 1) * CA_HEAD_DIM)
        k_ref[:, sl] = _rms(kv[:, sl], gk).astype(bf16)
    v_ref[...] = kv[:, D_MODEL:].astype(bf16)


def _memory_kv(mem, norm_mem, wkv_b, k_norm):
    bsz, m, _ = mem.shape
    const = lambda b: (0, 0)
    blk = pl.BlockSpec((None, m, D_MODEL), lambda b: (b, 0, 0))
    return pl.pallas_call(
        _memkv_kernel, grid=(bsz,),
        in_specs=[blk, pl.BlockSpec((1, D_MODEL), const), pl.BlockSpec((D_MODEL, 2 * D_MODEL), const),
                  pl.BlockSpec((1, CA_HEAD_DIM), const)],
        out_specs=[blk, blk],
        out_shape=[jax.ShapeDtypeStruct((bsz, m, D_MODEL), bf16)] * 2,
        compiler_params=_params(1), name="memory_kv",
    )(mem, norm_mem.reshape(1, -1), wkv_b, k_norm.reshape(1, -1))


def _outca_kernel(x_ref, y0_ref, y1_ref, y2_ref, y3_ref, w0_ref, w1_ref, w2_ref, w3_ref, gca_ref, wq_ref, kn_ref,
                  v_ref, gq_ref, wo_ref, o_ref):
    x1 = x_ref[...]
    for y_ref, w_ref in ((y0_ref, w0_ref), (y1_ref, w1_ref), (y2_ref, w2_ref), (y3_ref, w3_ref)):
        x1 = x1 + _dot(y_ref[...].astype(bf16), w_ref[...])
    h = _rms(x1, gca_ref[...]).astype(bf16)
    q = _dot(h, wq_ref[...])
    gq = gq_ref[...]
    outs = []
    for hh in range(CA_HEADS):
        sl = slice(hh * CA_HEAD_DIM, (hh + 1) * CA_HEAD_DIM)
        qn = (_rms(q[:, sl], gq) * (CA_HEAD_DIM ** -0.5)).astype(bf16)
        s = _dot_nt(qn, kn_ref[:, sl])
        m = jnp.max(s, axis=-1, keepdims=True)
        p = jnp.exp(s - m)
        l = jnp.sum(p, axis=-1, keepdims=True)
        outs.append((_dot(p.astype(bf16), v_ref[:, sl]) / l).astype(bf16))
    o_ref[...] = x1 + _dot(jnp.concatenate(outs, axis=-1), wo_ref[...])


def _outproj_cross_attention(x2d, ys, w_out_b, bsz, seq, norm_ca, wq_b, kn, vv, q_norm, wo_b):
    t = bsz * seq
    tm = TM_CA
    ns = seq // tm
    m = kn.shape[1]
    row = lambda b, i: (b * ns + i, 0)
    const = lambda b, i: (0, 0)
    mix = pl.BlockSpec((tm, GROUP_WIDTH), row)
    wblk = pl.BlockSpec((GROUP_WIDTH, D_MODEL), const)
    sq = pl.BlockSpec((D_MODEL, D_MODEL), const)
    kvb = pl.BlockSpec((None, m, D_MODEL), lambda b, i: (b, 0, 0))
    w_parts = [w_out_b[j * GROUP_WIDTH:(j + 1) * GROUP_WIDTH] for j in range(4)]
    return pl.pallas_call(
        _outca_kernel, grid=(bsz, ns),
        in_specs=[pl.BlockSpec((tm, D_MODEL), row), mix, mix, mix, mix, wblk, wblk, wblk, wblk,
                  pl.BlockSpec((1, D_MODEL), const), sq, kvb, kvb, pl.BlockSpec((1, CA_HEAD_DIM), const), sq],
        out_specs=pl.BlockSpec((tm, D_MODEL), row),
        out_shape=jax.ShapeDtypeStruct((t, D_MODEL), f32),
        compiler_params=_params(2), name="outproj_cross_attention",
    )(x2d, *ys, *w_parts, norm_ca.reshape(1, -1), wq_b, kn, vv, q_norm.reshape(1, -1), wo_b)


def _router_kernel(x_ref, g_ref, w1_ref, w2_ref, w3_ref, b_ref, slt_ref, h_ref, idx_ref, g1_ref, g2_ref, cnt_ref, base):
    @pl.when(pl.program_id(0) == 0)
    def _():
        base[...] = jnp.zeros_like(base)

    tm = TM_ROUTER
    h = _rms(x_ref[...], g_ref[...])
    h_ref[...] = h
    h1, h2, h3 = _split3(h)
    w1, w2, w3 = w1_ref[...], w2_ref[...], w3_ref[...]
    logits = (_dot(h1, w1) + _dot(h1, w2) + _dot(h2, w1) + _dot(h1, w3) + _dot(h3, w1) + _dot(h2, w2)) + b_ref[...]
    lg = logits[:, :LANES]
    le = logits[:, LANES:]
    lane = lax.broadcasted_iota(jnp.int32, (tm, LANES), 1)
    lanef = lane.astype(f32)
    neg = -jnp.inf

    lgm = jnp.where(lane < N_EXPERT_GROUPS, lg, neg)
    mg = jnp.max(lgm, axis=-1, keepdims=True)
    g_p = 1.0 / jnp.sum(jnp.exp(lgm - mg), axis=-1, keepdims=True)
    gi = jnp.min(jnp.where(lgm == mg, lanef, float(LANES)), axis=-1, keepdims=True)

    grp = (lane >> (EXPERTS_PER_GROUP.bit_length() - 1)).astype(f32)
    lem = jnp.where((grp == gi) & (lane < N_EXPERTS), le, neg)
    me = jnp.max(lem, axis=-1, keepdims=True)
    i1 = jnp.min(jnp.where(lem == me, lanef, float(LANES)), axis=-1, keepdims=True)
    le2 = jnp.where(lanef == i1, neg, lem)
    m2 = jnp.max(le2, axis=-1, keepdims=True)
    i2 = jnp.min(jnp.where(le2 == m2, lanef, float(LANES)), axis=-1, keepdims=True)
    e2 = jnp.exp(m2 - me)
    gate1 = g_p / (1.0 + e2)
    gate2 = g_p * e2 / (1.0 + e2)

    oh1 = (lanef == i1).astype(f32)
    oh2 = (lanef == i2).astype(f32)
    cnt = oh1 + oh2
    before = _dot(slt_ref[...], cnt.astype(bf16)) + base[...]
    r1 = jnp.sum(oh1 * before, axis=-1, keepdims=True)
    r2 = jnp.sum(oh2 * before, axis=-1, keepdims=True)
    total = base[...] + jnp.sum(cnt, axis=0, keepdims=True)
    base[...] = total
    cnt_ref[...] = total

    packed = jnp.where(lane == 0, i1, jnp.where(lane == 1, i2, jnp.where(lane == 2, r1, r2)))
    idx_ref[...] = packed.astype(jnp.int32)
    g1_ref[...] = jnp.broadcast_to(gate1, (tm, LANES))
    g2_ref[...] = jnp.broadcast_to(gate2, (tm, LANES))


def _router(x2d, norm_g, wg, bg, we, be):
    t = x2d.shape[0]
    tm = TM_ROUTER
    wr = jnp.zeros((D_MODEL, 2 * LANES), f32).at[:, :N_EXPERT_GROUPS].set(wg).at[:, LANES:LANES + N_EXPERTS].set(we)
    br = jnp.zeros((1, 2 * LANES), f32).at[0, :N_EXPERT_GROUPS].set(bg).at[0, LANES:LANES + N_EXPERTS].set(be)
    w1, w2, w3 = _split3(wr)
    slt = jnp.asarray(np.tril(np.ones((tm, tm), np.float32), -1), bf16)
    const = lambda i: (0, 0)
    row = lambda i: (i, 0)
    wspec = pl.BlockSpec((D_MODEL, 2 * LANES), const)
    meta = pl.BlockSpec((tm, LANES), row)
    return pl.pallas_call(
        _router_kernel, grid=(t // tm,),
        in_specs=[pl.BlockSpec((tm, D_MODEL), row), pl.BlockSpec((1, D_MODEL), const), wspec, wspec, wspec,
                  pl.BlockSpec((1, 2 * LANES), const), pl.BlockSpec((tm, tm), const)],
        out_specs=[pl.BlockSpec((tm, D_MODEL), row), meta, meta, meta, pl.BlockSpec((1, LANES), const)],
        out_shape=[jax.ShapeDtypeStruct((t, D_MODEL), f32), jax.ShapeDtypeStruct((t, LANES), jnp.int32),
                   jax.ShapeDtypeStruct((t, LANES), f32), jax.ShapeDtypeStruct((t, LANES), f32),
                   jax.ShapeDtypeStruct((1, LANES), f32)],
        scratch_shapes=[pltpu.VMEM((1, LANES), f32)],
        compiler_params=_params(1), name="moe_router",
    )(x2d, norm_g.reshape(1, -1), w1, w2, w3, br, slt)


def _row_copy(src, src_row, dst, dst_row, sem):
    return pltpu.make_async_copy(src.at[pl.ds(src_row, 1)], dst.at[pl.ds(dst_row, 1)], sem)


def _dispatch_kernel(starts_ref, e_ref, r_ref, h_ref, zero_ref, buf_ref, sem):
    del zero_ref
    n = TOP_K * TD_DISPATCH
    tok0 = pl.program_id(0) * TD_DISPATCH

    def issue(j, carry):
        dest = starts_ref[e_ref[j]] + r_ref[j]
        _row_copy(h_ref, tok0 + (j >> 1), buf_ref, dest, sem).start()
        return carry

    lax.fori_loop(0, n, issue, 0)

    def drain(j, carry):
        _row_copy(h_ref, 0, buf_ref, 0, sem).wait()
        return carry

    lax.fori_loop(0, n, drain, 0)


def _dispatch(h, e_flat, r_flat, starts_pad, p_rows):
    t = h.shape[0]
    n = TOP_K * TD_DISPATCH
    smem = pl.BlockSpec((n,), lambda i, s: (i,), memory_space=pltpu.SMEM)
    anyspec = pl.BlockSpec(memory_space=pl.ANY)
    grid_spec = pltpu.PrefetchScalarGridSpec(
        num_scalar_prefetch=1, grid=(t // TD_DISPATCH,), in_specs=[smem, smem, anyspec, anyspec], out_specs=anyspec,
        scratch_shapes=[pltpu.SemaphoreType.DMA(())])
    return pl.pallas_call(
        _dispatch_kernel, grid_spec=grid_spec, out_shape=jax.ShapeDtypeStruct((p_rows, D_MODEL), f32),
        input_output_aliases={4: 0}, compiler_params=_params(1), name="moe_dispatch",
    )(starts_pad, e_flat, r_flat, h, jnp.zeros((p_rows, D_MODEL), f32))


def _expert_kernel(blk_e_ref, nused_ref, x_ref, wg_ref, wu_ref, wd_ref, o_ref):
    del blk_e_ref
    used = pl.program_id(0) < nused_ref[0]

    @pl.when(used)
    def _():
        x = x_ref[...].astype(bf16)
        g = _dot(x, wg_ref[...])
        u = _dot(x, wu_ref[...])
        o_ref[...] = _dot((_silu(g) * u).astype(bf16), wd_ref[...])

    @pl.when(jnp.logical_not(used))
    def _():
        o_ref[...] = jnp.zeros_like(o_ref)


def _experts(buf, blk_e, n_used, wg_b, wu_b, wd_b):
    p_rows = buf.shape[0]
    xspec = pl.BlockSpec((MOE_BLOCK, D_MODEL), lambda i, be, nu: (i, 0))
    wspec = lambda a, b: pl.BlockSpec((None, a, b), lambda i, be, nu: (be[i], 0, 0))
    grid_spec = pltpu.PrefetchScalarGridSpec(
        num_scalar_prefetch=2, grid=(p_rows // MOE_BLOCK,),
        in_specs=[xspec, wspec(D_MODEL, EXPERT_HIDDEN), wspec(D_MODEL, EXPERT_HIDDEN), wspec(EXPERT_HIDDEN, D_MODEL)],
        out_specs=xspec)
    return pl.pallas_call(
        _expert_kernel, grid_spec=grid_spec, out_shape=jax.ShapeDtypeStruct((p_rows, D_MODEL), f32),
        compiler_params=_params(1), name="moe_experts",
    )(blk_e, n_used, buf, wg_b, wu_b, wd_b)


def _combine_kernel(starts_ref, e_ref, r_ref, x_ref, g1_ref, g2_ref, yb_ref, o_ref, rows0, rows1, sem):
    tc = TC_COMBINE

    def issue(tk, carry):
        j = 2 * tk
        _row_copy(yb_ref, starts_ref[e_ref[j]] + r_ref[j], rows0, tk, sem).start()
        _row_copy(yb_ref, starts_ref[e_ref[j + 1]] + r_ref[j + 1], rows1, tk, sem).start()
        return carry

    lax.fori_loop(0, tc, issue, 0)

    def drain(tk, carry):
        _row_copy(yb_ref, 0, rows0, 0, sem).wait()
        _row_copy(yb_ref, 0, rows1, 0, sem).wait()
        return carry

    lax.fori_loop(0, tc, drain, 0)
    reps = D_MODEL // LANES
    g1 = jnp.concatenate([g1_ref[...]] * reps, axis=-1)
    g2 = jnp.concatenate([g2_ref[...]] * reps, axis=-1)
    o_ref[...] = x_ref[...] + (g1 * rows0[...] + g2 * rows1[...])


def _combine(x2d, g1, g2, yb, e_flat, r_flat, starts_pad):
    t = x2d.shape[0]
    tc = TC_COMBINE
    n = TOP_K * tc
    smem = pl.BlockSpec((n,), lambda i, s: (i,), memory_space=pltpu.SMEM)
    row = lambda i, s: (i, 0)
    grid_spec = pltpu.PrefetchScalarGridSpec(
        num_scalar_prefetch=1, grid=(t // tc,),
        in_specs=[smem, smem, pl.BlockSpec((tc, D_MODEL), row), pl.BlockSpec((tc, LANES), row),
                  pl.BlockSpec((tc, LANES), row), pl.BlockSpec(memory_space=pl.ANY)],
        out_specs=pl.BlockSpec((tc, D_MODEL), row),
        scratch_shapes=[pltpu.VMEM((tc, D_MODEL), f32), pltpu.VMEM((tc, D_MODEL), f32), pltpu.SemaphoreType.DMA(())])
    return pl.pallas_call(
        _combine_kernel, grid_spec=grid_spec, out_shape=jax.ShapeDtypeStruct((t, D_MODEL), f32),
        compiler_params=_params(1), name="moe_combine",
    )(starts_pad, e_flat, r_flat, x2d, g1, g2, yb)


def _moe(x2d, norm_g, wg, bg, we, be, w_gate_b, w_up_b, w_down_b):
    t = x2d.shape[0]
    h, idx, g1, g2, cnt = _router(x2d, norm_g, wg, bg, we, be)
    counts = cnt[0, :N_EXPERTS].astype(jnp.int32)
    padded = ((counts + MOE_BLOCK - 1) // MOE_BLOCK) * MOE_BLOCK
    ends_pad = jnp.cumsum(padded)
    starts_pad = (ends_pad - padded).astype(jnp.int32)
    a = t * TOP_K
    p_rows = (-(-a // MOE_BLOCK)) * MOE_BLOCK + N_EXPERTS * MOE_BLOCK
    n_blocks = p_rows // MOE_BLOCK
    blk_e = jnp.minimum(jnp.searchsorted(ends_pad, jnp.arange(n_blocks) * MOE_BLOCK, side="right"),
                        N_EXPERTS - 1).astype(jnp.int32)
    n_used = (ends_pad[-1:] // MOE_BLOCK).astype(jnp.int32)
    e_flat = idx[:, 0:TOP_K].reshape(a)
    r_flat = idx[:, TOP_K:2 * TOP_K].reshape(a)
    buf = _dispatch(h, e_flat, r_flat, starts_pad, p_rows)
    yb = _experts(buf, blk_e, n_used, w_gate_b, w_up_b, w_down_b)
    return _combine(x2d, g1, g2, yb, e_flat, r_flat, starts_pad)


def _split_w_in(w_in):
    sizes = (GROUP_WIDTH, SSD_CONV_CH, SSD_HEADS) + (GROUP_WIDTH,) * 8
    names = ("z", "xbc", "dt", "sc_b", "sc_c", "sc_h", "q", "k", "v", "cf_a", "cf_g")
    parts, o = {}, 0
    for nme, s in zip(names, sizes):
        parts[nme] = w_in[:, o:o + s]
        o += s
    parts["dt"] = jnp.pad(parts["dt"], ((0, 0), (0, LANES - SSD_HEADS)))
    return [parts[nme].astype(bf16) for nme, _ in _SEGS]


def kernel(x, mem, norm_mix, w_in, ssd_conv_w, ssd_conv_b, ssd_dt_bias, ssd_a_log, ssd_d, ssd_norm, sc_conv_w,
           attn_q_norm, attn_k_norm, cf_conv_w, cf_conv_b, cf_ln_g, cf_ln_b, w_out, norm_ca, norm_mem, ca_wq, ca_wkv,
           ca_q_norm, ca_k_norm, ca_wo, norm_ffn, router_group_w, router_group_b, router_expert_w, router_expert_b,
           exp_w_gate, exp_w_up, exp_w_down):
    bsz, seq, d = x.shape
    depth = w_in.shape[0]
    assert d == D_MODEL and seq % ATT_TILE == 0 and seq % TS_SSD == 0 and (bsz * seq) % TM_PROJ == 0
    x2d = x.reshape(bsz * seq, d)
    for l in range(depth):
        proj = _in_projection(x2d, norm_mix[l].reshape(1, -1), _split_w_in(w_in[l]))
        y_ssd = _ssd(proj, bsz, seq, ssd_conv_w[l], ssd_conv_b[l], ssd_dt_bias[l], ssd_a_log[l], ssd_d[l], ssd_norm[l])
        y_sc, y_cf = _convs(proj, bsz, seq, sc_conv_w[l], cf_conv_w[l], cf_conv_b[l], cf_ln_g[l], cf_ln_b[l])
        y_att = _dilated_attention(proj, bsz, seq, attn_q_norm[l], attn_k_norm[l])
        kn, vv = _memory_kv(mem, norm_mem[l], ca_wkv[l].astype(bf16), ca_k_norm[l])
        x2d = _outproj_cross_attention(x2d, (y_ssd, y_sc, y_att, y_cf), w_out[l].astype(bf16), bsz, seq, norm_ca[l],
                                       ca_wq[l].astype(bf16), kn, vv, ca_q_norm[l], ca_wo[l].astype(bf16))
        x2d = _moe(x2d, norm_ffn[l], router_group_w[l], router_group_b[l], router_expert_w[l], router_expert_b[l],
                   exp_w_gate[l].astype(bf16), exp_w_up[l].astype(bf16), exp_w_down[l].astype(bf16))
    return x2d.reshape(bsz, seq, d)
```

```python
import functools

import numpy as np
import jax
import jax.numpy as jnp
from jax import lax
from jax.experimental import pallas as pl
from jax.experimental.pallas import tpu as pltpu

f32 = jnp.float32
bf16 = jnp.bfloat16

D_MODEL = 1024
GROUP_WIDTH = 384
SSD_HEADS = 6
SSD_HEAD_DIM = 64
SSD_GROUPS = 2
SSD_STATE = 128
SSD_CONV = 4
SSD_CHUNK = 128
SSD_CONV_CH = GROUP_WIDTH + 2 * SSD_GROUPS * SSD_STATE
SC_WIDTH = 3
ATT_HEADS = 6
ATT_HEAD_DIM = 64
ATT_BLOCK = 128
DILATIONS = (1, 4, 16)
ATT_TILE = ATT_BLOCK * max(DILATIONS)
ATT_GROUP = 4
CF_KERNEL = 31
CA_HEADS = 4
CA_HEAD_DIM = D_MODEL // CA_HEADS
N_EXPERT_GROUPS = 4
EXPERTS_PER_GROUP = 8
N_EXPERTS = N_EXPERT_GROUPS * EXPERTS_PER_GROUP
TOP_K = 2
EXPERT_HIDDEN = D_MODEL // 2
MOE_BLOCK = 256
RMS_EPS = 1e-6
LN_EPS = 1e-5

LANES = 128
SUBLANES = 8
VMEM_LIMIT_BYTES = 56 * 1024 * 1024

TM_PROJ = 512
TS_SSD = 1024
TS_CONV = 512
CONV_ROWS = 64
TM_CA = 512
TM_ROUTER = 512
TD_DISPATCH = 512
TC_COMBINE = 512
DMA_UNROLL = 8

_SEGS = (("xbc", SSD_CONV_CH), ("dt", LANES), ("z", GROUP_WIDTH), ("sc_b", GROUP_WIDTH), ("sc_c", GROUP_WIDTH),
         ("sc_h", GROUP_WIDTH), ("q", GROUP_WIDTH), ("k", GROUP_WIDTH), ("v", GROUP_WIDTH), ("cf_a", GROUP_WIDTH),
         ("cf_g", GROUP_WIDTH))


def _params(n_axes):
    return pltpu.CompilerParams(dimension_semantics=("arbitrary",) * n_axes, vmem_limit_bytes=VMEM_LIMIT_BYTES)


def _dot(a, b):
    return jnp.dot(a, b, preferred_element_type=f32)


def _dot_nt(a, b):
    return lax.dot_general(a, b, (((1,), (1,)), ((), ())), preferred_element_type=f32)


def _split2(x):
    hi = x.astype(bf16)
    lo = (x - hi.astype(f32)).astype(bf16)
    return hi, lo


def _split3(x):
    hi = x.astype(bf16)
    r = x - hi.astype(f32)
    mid = r.astype(bf16)
    lo = (r - mid.astype(f32)).astype(bf16)
    return hi, mid, lo


def _dot_exact_rhs(x, m_bf16, n_split):
    parts = _split3(x) if n_split == 3 else _split2(x)
    acc = _dot(parts[0], m_bf16)
    for p in parts[1:]:
        acc = acc + _dot(p, m_bf16)
    return acc


def _sigmoid(x):
    return 1.0 / (1.0 + jnp.exp(-x))


def _silu(x):
    return x * _sigmoid(x)


def _rms(x, gain):
    ms = jnp.mean(x * x, axis=-1, keepdims=True)
    return x * lax.rsqrt(ms + RMS_EPS) * gain


def _inproj_kernel(x_ref, g_ref, *refs):
    n = len(_SEGS)
    w_refs, o_refs = refs[:n], refs[n:]
    h = _rms(x_ref[...], g_ref[...]).astype(bf16)
    for w_ref, o_ref in zip(w_refs, o_refs):
        o_ref[...] = _dot(h, w_ref[...])


def _in_projection(x2d, gain, w_segs):
    t = x2d.shape[0]
    tm = TM_PROJ
    in_specs = [pl.BlockSpec((tm, D_MODEL), lambda i: (i, 0)), pl.BlockSpec((1, D_MODEL), lambda i: (0, 0))]
    in_specs += [pl.BlockSpec((D_MODEL, w), lambda i: (0, 0)) for _, w in _SEGS]
    out_specs = [pl.BlockSpec((tm, w), lambda i: (i, 0)) for _, w in _SEGS]
    out_shape = [jax.ShapeDtypeStruct((t, w), f32) for _, w in _SEGS]
    outs = pl.pallas_call(
        _inproj_kernel, grid=(t // tm,), in_specs=in_specs, out_specs=out_specs, out_shape=out_shape,
        compiler_params=_params(1), name="in_projection",
    )(x2d, gain, *w_segs)
    return dict(zip([n for n, _ in _SEGS], outs))


def _ssd_kernel(xbc_ref, dt_ref, z_ref, cw_ref, cb_ref, dtb_ref, alog_ref, dvec_ref, ng_ref, tril_ref, e384_ref,
                e768_ref, o_ref, xbuf, xc, state):
    ts = TS_SSD
    q = SSD_CHUNK
    halo = SUBLANES

    @pl.when(pl.program_id(1) == 0)
    def _():
        xbuf[0:halo, :] = jnp.zeros((halo, SSD_CONV_CH), f32)
        state[...] = jnp.zeros_like(state)

    xbuf[halo:halo + ts, :] = xbc_ref[...]
    acc = cb_ref[...] + cw_ref[0:1, :] * xbuf[halo - 3:halo - 3 + ts, :]
    for k in range(1, SSD_CONV):
        acc = acc + cw_ref[k:k + 1, :] * xbuf[halo - 3 + k:halo - 3 + k + ts, :]
    xc[...] = _silu(acc)
    xbuf[0:halo, :] = xbuf[ts:ts + halo, :]

    lane_row = lax.broadcasted_iota(jnp.int32, (1, LANES), 1)
    head_lane = lane_row < SSD_HEADS
    a_row = jnp.where(head_lane, -jnp.exp(alog_ref[...]), 0.0)
    lane_sq = lax.broadcasted_iota(jnp.int32, (q, LANES), 1)
    row_sq = lax.broadcasted_iota(jnp.int32, (q, LANES), 0)
    causal = row_sq >= lane_sq
    left = lane_sq < SSD_HEAD_DIM
    tril = tril_ref[...]
    e384 = e384_ref[...]
    e768 = e768_ref[...]
    dvec = dvec_ref[...]
    ng = ng_ref[...]

    def chunk(c, carry):
        r0 = pl.multiple_of(c * q, q)
        dtr = dt_ref[pl.ds(r0, q), :] + dtb_ref[...]
        sp = jnp.maximum(dtr, 0.0) + jnp.log1p(jnp.exp(-jnp.abs(dtr)))
        dt = jnp.where(head_lane, sp, 0.0)
        da = dt * a_row
        d3 = _split3(da)
        cs = _dot(tril, d3[0]) + _dot(tril, d3[1]) + _dot(tril, d3[2])
        cs_t = cs.T
        c3 = _split3(cs)
        cs384 = _dot(c3[0], e384) + _dot(c3[1], e384) + _dot(c3[2], e384)
        cs768 = _dot(c3[0], e768) + _dot(c3[1], e768) + _dot(c3[2], e768)
        dt384 = _dot_exact_rhs(dt, e384, 2)
        last384 = cs384[q - 1:q, :]
        decay384 = jnp.exp(last384 - cs384)
        expcs384 = jnp.exp(cs384)

        xs = xc[pl.ds(r0, q), 0:GROUP_WIDTH]
        bm = xc[pl.ds(r0, q), GROUP_WIDTH:GROUP_WIDTH + SSD_GROUPS * SSD_STATE]
        cm = xc[pl.ds(r0, q), GROUP_WIDTH + SSD_GROUPS * SSD_STATE:SSD_CONV_CH]
        xdt = xs * dt384
        xdt_b = xdt.astype(bf16)
        xdec_b = (xdt * decay384).astype(bf16)
        st = state[...]
        st_b = st.astype(bf16)

        cb_g, bt_g, c_g = [], [], []
        for g in range(SSD_GROUPS):
            b_f = bm[:, g * SSD_STATE:(g + 1) * SSD_STATE]
            c_b = cm[:, g * SSD_STATE:(g + 1) * SSD_STATE].astype(bf16)
            cb_g.append(_dot_nt(c_b, b_f.astype(bf16)))
            bt_g.append(b_f.T.astype(bf16))
            c_g.append(c_b)

        y_parts, st_parts = [], []
        for p in range(SSD_HEADS // 2):
            sl = slice(p * LANES, (p + 1) * LANES)
            xdt_p, xdec_p, st_p = xdt_b[:, sl], xdec_b[:, sl], st_b[:, sl]
            yd, s_new, yo = [], {}, {}
            for hh in range(2):
                h = 2 * p + hh
                g = h // (SSD_HEADS // SSD_GROUPS)
                diff = cs768[:, h * LANES:(h + 1) * LANES] - cs_t[h:h + 1, :]
                seg = jnp.exp(jnp.where(causal, diff, -1e30))
                m = (cb_g[g] * seg).astype(bf16)
                yd.append(_dot(m, xdt_p))
                if g not in s_new:
                    s_new[g] = _dot(bt_g[g], xdec_p)
                    yo[g] = _dot(c_g[g], st_p)
            g0 = (2 * p) // (SSD_HEADS // SSD_GROUPS)
            g1 = (2 * p + 1) // (SSD_HEADS // SSD_GROUPS)
            y_diag = jnp.where(left, yd[0], yd[1])
            y_off = jnp.where(left, yo[g0], yo[g1])
            s_pair = jnp.where(left, s_new[g0], s_new[g1])
            y_parts.append(y_diag + y_off * expcs384[:, sl] + xs[:, sl] * dvec[:, sl])
            st_parts.append(st[:, sl] * expcs384[q - 1:q, sl] + s_pair)
        state[...] = jnp.concatenate(st_parts, axis=-1)
        y = jnp.concatenate(y_parts, axis=-1)
        yg = y * _silu(z_ref[pl.ds(r0, q), :])
        o_ref[pl.ds(r0, q), :] = _rms(yg, ng)
        return carry

    lax.fori_loop(0, ts // q, chunk, 0)


def _ssd_constants():
    q = SSD_CHUNK
    tril = np.tril(np.ones((q, q), np.float32))
    e384 = np.zeros((LANES, GROUP_WIDTH), np.float32)
    e768 = np.zeros((LANES, SSD_HEADS * LANES), np.float32)
    for h in range(SSD_HEADS):
        e384[h, h * SSD_HEAD_DIM:(h + 1) * SSD_HEAD_DIM] = 1.0
        e768[h, h * LANES:(h + 1) * LANES] = 1.0
    return jnp.asarray(tril, bf16), jnp.asarray(e384, bf16), jnp.asarray(e768, bf16)


def _ssd(proj, bsz, seq, conv_w, conv_b, dt_bias, a_log, d_skip, norm_g):
    t = bsz * seq
    ts = TS_SSD
    ns = seq // ts
    tril, e384, e768 = _ssd_constants()
    pad = LANES - SSD_HEADS
    dtb = jnp.pad(dt_bias, (0, pad)).reshape(1, LANES)
    alog = jnp.pad(a_log, (0, pad)).reshape(1, LANES)
    dvec = jnp.repeat(d_skip, SSD_HEAD_DIM).reshape(1, GROUP_WIDTH)
    row = lambda b, i: (b * ns + i, 0)
    const = lambda b, i: (0, 0)
    return pl.pallas_call(
        _ssd_kernel, grid=(bsz, ns),
        in_specs=[pl.BlockSpec((ts, SSD_CONV_CH), row), pl.BlockSpec((ts, LANES), row), pl.BlockSpec((ts, GROUP_WIDTH), row),
                  pl.BlockSpec((SSD_CONV, SSD_CONV_CH), const), pl.BlockSpec((1, SSD_CONV_CH), const),
                  pl.BlockSpec((1, LANES), const), pl.BlockSpec((1, LANES), const), pl.BlockSpec((1, GROUP_WIDTH), const),
                  pl.BlockSpec((1, GROUP_WIDTH), const), pl.BlockSpec((SSD_CHUNK, SSD_CHUNK), const),
                  pl.BlockSpec((LANES, GROUP_WIDTH), const), pl.BlockSpec((LANES, SSD_HEADS * LANES), const)],
        out_specs=pl.BlockSpec((ts, GROUP_WIDTH), row),
        out_shape=jax.ShapeDtypeStruct((t, GROUP_WIDTH), f32),
        scratch_shapes=[pltpu.VMEM((ts + SUBLANES, SSD_CONV_CH), f32), pltpu.VMEM((ts, SSD_CONV_CH), f32),
                        pltpu.VMEM((SSD_STATE, GROUP_WIDTH), f32)],
        compiler_params=_params(2), name="ssd_scan",
    )(proj["xbc"], proj["dt"], proj["z"], conv_w, conv_b.reshape(1, -1), dtb, alog, dvec, norm_g.reshape(1, -1), tril,
      e384, e768)


_CF_HALO = 32


def _conv_kernel(scb_ref, scc_ref, sch_ref, cfa_ref, cfg_ref, scw_ref, cfw_ref, cfb_ref, lng_ref, lnb_ref, ysc_ref,
                 ycf_ref, ubuf, gbuf):
    ts = TS_CONV
    uh = SUBLANES

    @pl.when(pl.program_id(1) == 0)
    def _():
        ubuf[0:uh, :] = jnp.zeros((uh, GROUP_WIDTH), f32)
        gbuf[0:_CF_HALO, :] = jnp.zeros((_CF_HALO, GROUP_WIDTH), f32)

    ubuf[uh:uh + ts, :] = scc_ref[...] * sch_ref[...]
    gbuf[_CF_HALO:_CF_HALO + ts, :] = cfa_ref[...] * _sigmoid(cfg_ref[...])

    o = uh - (SC_WIDTH - 1)
    acc = scw_ref[0:1, :] * ubuf[o:o + ts, :]
    for k in range(1, SC_WIDTH):
        acc = acc + scw_ref[k:k + 1, :] * ubuf[o + k:o + k + ts, :]
    ysc_ref[...] = scb_ref[...] * acc

    o = _CF_HALO - (CF_KERNEL - 1)
    cfb = cfb_ref[...]
    lng = lng_ref[...]
    lnb = lnb_ref[...]
    for c in range(ts // CONV_ROWS):
        r0 = c * CONV_ROWS
        acc = cfb
        for s in range(SUBLANES):
            rows = CONV_ROWS + (SUBLANES if s else 0)
            part = None
            for k in range(CF_KERNEL):
                if (o + k) % SUBLANES != s:
                    continue
                a = r0 + o + k - s
                term = cfw_ref[k:k + 1, :] * gbuf[a:a + rows, :]
                part = term if part is None else part + term
            acc = acc + part[s:s + CONV_ROWS, :]
        mu = jnp.mean(acc, axis=-1, keepdims=True)
        xc = acc - mu
        var = jnp.mean(xc * xc, axis=-1, keepdims=True)
        y = xc * lax.rsqrt(var + LN_EPS) * lng + lnb
        ycf_ref[r0:r0 + CONV_ROWS, :] = _silu(y)

    ubuf[0:uh, :] = ubuf[ts:ts + uh, :]
    gbuf[0:_CF_HALO, :] = gbuf[ts:ts + _CF_HALO, :]


def _convs(proj, bsz, seq, sc_w, cf_w, cf_b, ln_g, ln_b):
    t = bsz * seq
    ts = TS_CONV
    ns = seq // ts
    row = lambda b, i: (b * ns + i, 0)
    const = lambda b, i: (0, 0)
    act = pl.BlockSpec((ts, GROUP_WIDTH), row)
    vec = pl.BlockSpec((1, GROUP_WIDTH), const)
    return pl.pallas_call(
        _conv_kernel, grid=(bsz, ns),
        in_specs=[act, act, act, act, act, pl.BlockSpec((SC_WIDTH, GROUP_WIDTH), const),
                  pl.BlockSpec((CF_KERNEL, GROUP_WIDTH), const), vec, vec, vec],
        out_specs=[act, act],
        out_shape=[jax.ShapeDtypeStruct((t, GROUP_WIDTH), f32)] * 2,
        scratch_shapes=[pltpu.VMEM((ts + SUBLANES, GROUP_WIDTH), f32), pltpu.VMEM((ts + _CF_HALO, GROUP_WIDTH), f32)],
        compiler_params=_params(2), name="gated_convs",
    )(proj["sc_b"], proj["sc_c"], proj["sc_h"], proj["cf_a"], proj["cf_g"], sc_w, cf_w, cf_b.reshape(1, -1),
      ln_g.reshape(1, -1), ln_b.reshape(1, -1))


def _attn_kernel(q_ref, k_ref, v_ref, qg_ref, kg_ref, bsum_ref, o_ref, qbuf, kbuf, vbuf, acc_o, acc_m, acc_l):
    ta = ATT_TILE
    n = ATT_BLOCK
    tile = pl.program_id(2)

    @pl.when(tile == 0)
    def _():
        kbuf[0:ta, :] = jnp.zeros((ta, LANES), f32)
        vbuf[0:ta, :] = jnp.zeros((ta, LANES), f32)

    bsum = bsum_ref[...]

    def head_rms(x, g):
        ms = _dot_exact_rhs(x * x, bsum, 2)
        return x * lax.rsqrt(ms + RMS_EPS) * g

    qbuf[...] = head_rms(q_ref[...], qg_ref[...]) * (ATT_HEAD_DIM ** -0.5)
    kbuf[ta:2 * ta, :] = head_rms(k_ref[...], kg_ref[...])
    vbuf[ta:2 * ta, :] = v_ref[...]

    qq = lax.broadcasted_iota(jnp.int32, (n, 2 * n), 0)
    kk = lax.broadcasted_iota(jnp.int32, (n, 2 * n), 1)
    rel = qq + n - kk
    band = (rel >= 0) & (rel <= n)
    cur = kk >= n
    lane = lax.broadcasted_iota(jnp.int32, (n, LANES), 1)
    left = lane < ATT_HEAD_DIM

    for d in DILATIONS:
        shift = d.bit_length() - 1

        def rows(start, size, d=d):
            return pl.ds(start, size) if d == 1 else pl.ds(start, size, stride=d)

        def body(it, carry, d=d, shift=shift, rows=rows):
            loaded = []
            for g in range(ATT_GROUP):
                idx = it * ATT_GROUP + g
                nb = idx >> shift
                r = idx & (d - 1)
                qs = nb * (n * d) + r
                ks = ta + qs - n * d
                q_r = qbuf[rows(qs, n), :]
                k_r = kbuf[rows(ks, 2 * n), :].astype(bf16)
                v_r = vbuf[rows(ks, 2 * n), :].astype(bf16)
                old = None
                if d != 1:
                    old = (acc_o[rows(qs, n), :], acc_m[rows(qs, n), :], acc_l[rows(qs, n), :])
                valid = band & (cur | ((tile > 0) | (nb > 0)))
                loaded.append((qs, q_r, k_r, v_r, old, valid))
            merged = []
            for qs, q_r, k_r, v_r, old, valid in loaded:
                o_h, m_h, l_h = [], [], []
                for hh in range(2):
                    qm = jnp.where(left if hh == 0 else ~left, q_r, 0.0).astype(bf16)
                    s = jnp.where(valid, _dot_nt(qm, k_r), -1e30)
                    m = jnp.max(s, axis=-1, keepdims=True)
                    p = jnp.exp(s - m)
                    l_h.append(jnp.sum(p, axis=-1, keepdims=True))
                    m_h.append(m)
                    o_h.append(_dot(p.astype(bf16), v_r))
                o_new = jnp.where(left, o_h[0], o_h[1])
                m_new = jnp.where(left, m_h[0], m_h[1])
                l_new = jnp.where(left, l_h[0], l_h[1])
                if old is not None:
                    o_old, m_old, l_old = old
                    m_tot = jnp.maximum(m_old, m_new)
                    a = jnp.exp(m_old - m_tot)
                    b = jnp.exp(m_new - m_tot)
                    o_new = a * o_old + b * o_new
                    l_new = a * l_old + b * l_new
                    m_new = m_tot
                merged.append((qs, o_new, m_new, l_new))
            for qs, o_new, m_new, l_new in merged:
                acc_o[rows(qs, n), :] = o_new
                acc_m[rows(qs, n), :] = jnp.broadcast_to(m_new, (n, LANES))
                acc_l[rows(qs, n), :] = jnp.broadcast_to(l_new, (n, LANES))
            return carry

        lax.fori_loop(0, ta // n // ATT_GROUP, body, 0)

    o_ref[...] = acc_o[...] / acc_l[...]
    kbuf[0:ta, :] = kbuf[ta:2 * ta, :]
    vbuf[0:ta, :] = vbuf[ta:2 * ta, :]


def _dilated_attention(proj, bsz, seq, q_norm, k_norm):
    t = bsz * seq
    ta = ATT_TILE
    nt = seq // ta
    pairs = ATT_HEADS // 2
    bsum = np.zeros((LANES, LANES), np.float32)
    for h in range(2):
        bsum[h * ATT_HEAD_DIM:(h + 1) * ATT_HEAD_DIM, h * ATT_HEAD_DIM:(h + 1) * ATT_HEAD_DIM] = 1.0 / ATT_HEAD_DIM
    blk = pl.BlockSpec((ta, LANES), lambda b, p, i: (b * nt + i, p))
    const = lambda b, p, i: (0, 0)
    vec = pl.BlockSpec((1, LANES), const)
    buf = lambda rows: pltpu.VMEM((rows, LANES), f32)
    return pl.pallas_call(
        _attn_kernel, grid=(bsz, pairs, nt),
        in_specs=[blk, blk, blk, vec, vec, pl.BlockSpec((LANES, LANES), const)],
        out_specs=blk,
        out_shape=jax.ShapeDtypeStruct((t, GROUP_WIDTH), f32),
        scratch_shapes=[buf(ta), buf(2 * ta), buf(2 * ta), buf(ta), buf(ta), buf(ta)],
        compiler_params=_params(3), name="dilated_attention",
    )(proj["q"], proj["k"], proj["v"], jnp.tile(q_norm, 2).reshape(1, LANES), jnp.tile(k_norm, 2).reshape(1, LANES),
      jnp.asarray(bsum, bf16))


def _memkv_kernel(mem_ref, gm_ref, wkv_ref, gk_ref, k_ref, v_ref):
    hm = _rms(mem_ref[...], gm_ref[...]).astype(bf16)
    kv = _dot(hm, wkv_ref[...])
    gk = gk_ref[...]
    for hh in range(CA_HEADS):
        sl = slice(hh * CA_HEAD_DIM, (hh + 1) * CA_HEAD_DIM)
        k_ref[:, sl] = _rms(kv[:, sl], gk).astype(bf16)
    v_ref[...] = kv[:, D_MODEL:].astype(bf16)


def _memory_kv(mem, norm_mem, wkv_b, k_norm):
    bsz, m, _ = mem.shape
    const = lambda b: (0, 0)
    blk = pl.BlockSpec((None, m, D_MODEL), lambda b: (b, 0, 0))
    return pl.pallas_call(
        _memkv_kernel, grid=(bsz,),
        in_specs=[blk, pl.BlockSpec((1, D_MODEL), const), pl.BlockSpec((D_MODEL, 2 * D_MODEL), const),
                  pl.BlockSpec((1, CA_HEAD_DIM), const)],
        out_specs=[blk, blk],
        out_shape=[jax.ShapeDtypeStruct((bsz, m, D_MODEL), bf16)] * 2,
        compiler_params=_params(1), name="memory_kv",
    )(mem, norm_mem.reshape(1, -1), wkv_b, k_norm.reshape(1, -1))


def _outca_kernel(x_ref, y0_ref, y1_ref, y2_ref, y3_ref, w0_ref, w1_ref, w2_ref, w3_ref, gca_ref, wq_ref, kn_ref,
                  v_ref, gq_ref, wo_ref, o_ref):
    x1 = x_ref[...]
    for y_ref, w_ref in ((y0_ref, w0_ref), (y1_ref, w1_ref), (y2_ref, w2_ref), (y3_ref, w3_ref)):
        x1 = x1 + _dot(y_ref[...].astype(bf16), w_ref[...])
    h = _rms(x1, gca_ref[...]).astype(bf16)
    q = _dot(h, wq_ref[...])
    gq = gq_ref[...]
    outs = []
    for hh in range(CA_HEADS):
        sl = slice(hh * CA_HEAD_DIM, (hh + 1) * CA_HEAD_DIM)
        qn = (_rms(q[:, sl], gq) * (CA_HEAD_DIM ** -0.5)).astype(bf16)
        s = _dot_nt(qn, kn_ref[:, sl])
        m = jnp.max(s, axis=-1, keepdims=True)
        p = jnp.exp(s - m)
        l = jnp.sum(p, axis=-1, keepdims=True)
        outs.append((_dot(p.astype(bf16), v_ref[:, sl]) / l).astype(bf16))
    o_ref[...] = x1 + _dot(jnp.concatenate(outs, axis=-1), wo_ref[...])


def _outproj_cross_attention(x2d, ys, w_out_b, bsz, seq, norm_ca, wq_b, kn, vv, q_norm, wo_b):
    t = bsz * seq
    tm = TM_CA
    ns = seq // tm
    m = kn.shape[1]
    row = lambda b, i: (b * ns + i, 0)
    const = lambda b, i: (0, 0)
    mix = pl.BlockSpec((tm, GROUP_WIDTH), row)
    wblk = pl.BlockSpec((GROUP_WIDTH, D_MODEL), const)
    sq = pl.BlockSpec((D_MODEL, D_MODEL), const)
    kvb = pl.BlockSpec((None, m, D_MODEL), lambda b, i: (b, 0, 0))
    w_parts = [w_out_b[j * GROUP_WIDTH:(j + 1) * GROUP_WIDTH] for j in range(4)]
    return pl.pallas_call(
        _outca_kernel, grid=(bsz, ns),
        in_specs=[pl.BlockSpec((tm, D_MODEL), row), mix, mix, mix, mix, wblk, wblk, wblk, wblk,
                  pl.BlockSpec((1, D_MODEL), const), sq, kvb, kvb, pl.BlockSpec((1, CA_HEAD_DIM), const), sq],
        out_specs=pl.BlockSpec((tm, D_MODEL), row),
        out_shape=jax.ShapeDtypeStruct((t, D_MODEL), f32),
        compiler_params=_params(2), name="outproj_cross_attention",
    )(x2d, *ys, *w_parts, norm_ca.reshape(1, -1), wq_b, kn, vv, q_norm.reshape(1, -1), wo_b)


def _router_kernel(x_ref, g_ref, w1_ref, w2_ref, b_ref, slt_ref, h_ref, idx_ref, g1_ref, g2_ref, cnt_ref, base):
    @pl.when(pl.program_id(0) == 0)
    def _():
        base[...] = jnp.zeros_like(base)

    tm = TM_ROUTER
    h = _rms(x_ref[...], g_ref[...])
    h_ref[...] = h
    h1, h2 = _split2(h)
    w1 = w1_ref[...]
    logits = (_dot(h1, w1) + _dot(h1, w2_ref[...]) + _dot(h2, w1)) + b_ref[...]
    lane = lax.broadcasted_iota(jnp.int32, (tm, LANES), 1)
    lanef = lane.astype(f32)
    neg = -jnp.inf

    lgm = jnp.where((lane >= N_EXPERTS) & (lane < N_EXPERTS + N_EXPERT_GROUPS), logits, neg)
    mg = jnp.max(lgm, axis=-1, keepdims=True)
    g_p = 1.0 / jnp.sum(jnp.exp(lgm - mg), axis=-1, keepdims=True)
    gi = jnp.min(jnp.where(lgm == mg, lanef, float(LANES)), axis=-1, keepdims=True) - float(N_EXPERTS)

    grp = (lane >> (EXPERTS_PER_GROUP.bit_length() - 1)).astype(f32)
    lem = jnp.where((grp == gi) & (lane < N_EXPERTS), logits, neg)
    me = jnp.max(lem, axis=-1, keepdims=True)
    i1 = jnp.min(jnp.where(lem == me, lanef, float(LANES)), axis=-1, keepdims=True)
    le2 = jnp.where(lanef == i1, neg, lem)
    m2 = jnp.max(le2, axis=-1, keepdims=True)
    i2 = jnp.min(jnp.where(le2 == m2, lanef, float(LANES)), axis=-1, keepdims=True)
    e2 = jnp.exp(m2 - me)
    gate1 = g_p / (1.0 + e2)
    gate2 = g_p * e2 / (1.0 + e2)

    oh1 = (lanef == i1).astype(f32)
    oh2 = (lanef == i2).astype(f32)
    cnt = oh1 + oh2
    before = _dot(slt_ref[...], cnt.astype(bf16)) + base[...]
    r1 = jnp.sum(oh1 * before, axis=-1, keepdims=True)
    r2 = jnp.sum(oh2 * before, axis=-1, keepdims=True)
    total = base[...] + jnp.sum(cnt, axis=0, keepdims=True)
    base[...] = total
    cnt_ref[...] = total

    packed = jnp.where(lane == 0, i1, jnp.where(lane == 1, i2, jnp.where(lane == 2, r1, r2)))
    idx_ref[...] = packed.astype(jnp.int32)
    g1_ref[...] = jnp.broadcast_to(gate1, (tm, LANES))
    g2_ref[...] = jnp.broadcast_to(gate2, (tm, LANES))


def _router(x2d, norm_g, wg, bg, we, be):
    t = x2d.shape[0]
    tm = TM_ROUTER
    pad = LANES - N_EXPERTS - N_EXPERT_GROUPS
    wr = jnp.pad(jnp.concatenate([we, wg], axis=1), ((0, 0), (0, pad)))
    br = jnp.pad(jnp.concatenate([be, bg]), (0, pad)).reshape(1, LANES)
    w1, w2 = _split2(wr)
    slt = jnp.asarray(np.tril(np.ones((tm, tm), np.float32), -1), bf16)
    const = lambda i: (0, 0)
    row = lambda i: (i, 0)
    wspec = pl.BlockSpec((D_MODEL, LANES), const)
    meta = pl.BlockSpec((tm, LANES), row)
    return pl.pallas_call(
        _router_kernel, grid=(t // tm,),
        in_specs=[pl.BlockSpec((tm, D_MODEL), row), pl.BlockSpec((1, D_MODEL), const), wspec, wspec,
                  pl.BlockSpec((1, LANES), const), pl.BlockSpec((tm, tm), const)],
        out_specs=[pl.BlockSpec((tm, D_MODEL), row), meta, meta, meta, pl.BlockSpec((1, LANES), const)],
        out_shape=[jax.ShapeDtypeStruct((t, D_MODEL), f32), jax.ShapeDtypeStruct((t, LANES), jnp.int32),
                   jax.ShapeDtypeStruct((t, LANES), f32), jax.ShapeDtypeStruct((t, LANES), f32),
                   jax.ShapeDtypeStruct((1, LANES), f32)],
        scratch_shapes=[pltpu.VMEM((1, LANES), f32)],
        compiler_params=_params(1), name="moe_router",
    )(x2d, norm_g.reshape(1, -1), w1, w2, br, slt)


def _row_copy(src, src_row, dst, dst_row, sem):
    return pltpu.make_async_copy(src.at[pl.ds(src_row, 1)], dst.at[pl.ds(dst_row, 1)], sem)


def _dispatch_kernel(starts_ref, e_ref, r_ref, h_ref, zero_ref, buf_ref, sem):
    del zero_ref
    n = TOP_K * TD_DISPATCH

    def issue(j, carry):
        dest = starts_ref[e_ref[j]] + r_ref[j]
        _row_copy(h_ref, j >> 1, buf_ref, dest, sem).start()
        return carry

    lax.fori_loop(0, n, issue, 0, unroll=DMA_UNROLL)

    def drain(j, carry):
        _row_copy(h_ref, 0, buf_ref, 0, sem).wait()
        return carry

    lax.fori_loop(0, n, drain, 0, unroll=DMA_UNROLL)


def _dispatch(h, e_flat, r_flat, starts_pad, p_rows):
    t = h.shape[0]
    n = TOP_K * TD_DISPATCH
    smem = pl.BlockSpec((n,), lambda i, s: (i,), memory_space=pltpu.SMEM)
    anyspec = pl.BlockSpec(memory_space=pl.ANY)
    hspec = pl.BlockSpec((TD_DISPATCH, D_MODEL), lambda i, s: (i, 0))
    grid_spec = pltpu.PrefetchScalarGridSpec(
        num_scalar_prefetch=1, grid=(t // TD_DISPATCH,), in_specs=[smem, smem, hspec, anyspec], out_specs=anyspec,
        scratch_shapes=[pltpu.SemaphoreType.DMA(())])
    return pl.pallas_call(
        _dispatch_kernel, grid_spec=grid_spec, out_shape=jax.ShapeDtypeStruct((p_rows, D_MODEL), f32),
        input_output_aliases={4: 0}, compiler_params=_params(1), name="moe_dispatch",
    )(starts_pad, e_flat, r_flat, h, jnp.zeros((p_rows, D_MODEL), f32))


def _expert_kernel(blk_e_ref, nused_ref, x_ref, wg_ref, wu_ref, wd_ref, o_ref):
    del blk_e_ref
    used = pl.program_id(0) < nused_ref[0]

    @pl.when(used)
    def _():
        x = x_ref[...].astype(bf16)
        g = _dot(x, wg_ref[...].astype(bf16))
        u = _dot(x, wu_ref[...].astype(bf16))
        o_ref[...] = _dot((_silu(g) * u).astype(bf16), wd_ref[...].astype(bf16))

    @pl.when(jnp.logical_not(used))
    def _():
        o_ref[...] = jnp.zeros_like(o_ref)


def _experts(buf, blk_e, n_used, wg_b, wu_b, wd_b):
    p_rows = buf.shape[0]
    xspec = pl.BlockSpec((MOE_BLOCK, D_MODEL), lambda i, be, nu: (i, 0))
    wspec = lambda a, b: pl.BlockSpec((None, a, b), lambda i, be, nu: (be[i], 0, 0))
    grid_spec = pltpu.PrefetchScalarGridSpec(
        num_scalar_prefetch=2, grid=(p_rows // MOE_BLOCK,),
        in_specs=[xspec, wspec(D_MODEL, EXPERT_HIDDEN), wspec(D_MODEL, EXPERT_HIDDEN), wspec(EXPERT_HIDDEN, D_MODEL)],
        out_specs=xspec)
    return pl.pallas_call(
        _expert_kernel, grid_spec=grid_spec, out_shape=jax.ShapeDtypeStruct((p_rows, D_MODEL), f32),
        compiler_params=_params(1), name="moe_experts",
    )(blk_e, n_used, buf, wg_b, wu_b, wd_b)


def _combine_kernel(starts_ref, e_ref, r_ref, x_ref, g1_ref, g2_ref, yb_ref, o_ref, rows0, rows1, sem):
    tc = TC_COMBINE

    def issue(tk, carry):
        j = 2 * tk
        _row_copy(yb_ref, starts_ref[e_ref[j]] + r_ref[j], rows0, tk, sem).start()
        _row_copy(yb_ref, starts_ref[e_ref[j + 1]] + r_ref[j + 1], rows1, tk, sem).start()
        return carry

    lax.fori_loop(0, tc, issue, 0, unroll=DMA_UNROLL)

    def drain(tk, carry):
        _row_copy(yb_ref, 0, rows0, 0, sem).wait()
        _row_copy(yb_ref, 0, rows1, 0, sem).wait()
        return carry

    lax.fori_loop(0, tc, drain, 0, unroll=DMA_UNROLL)
    reps = D_MODEL // LANES
    g1 = jnp.concatenate([g1_ref[...]] * reps, axis=-1)
    g2 = jnp.concatenate([g2_ref[...]] * reps, axis=-1)
    o_ref[...] = x_ref[...] + (g1 * rows0[...] + g2 * rows1[...])


def _combine(x2d, g1, g2, yb, e_flat, r_flat, starts_pad):
    t = x2d.shape[0]
    tc = TC_COMBINE
    n = TOP_K * tc
    smem = pl.BlockSpec((n,), lambda i, s: (i,), memory_space=pltpu.SMEM)
    row = lambda i, s: (i, 0)
    grid_spec = pltpu.PrefetchScalarGridSpec(
        num_scalar_prefetch=1, grid=(t // tc,),
        in_specs=[smem, smem, pl.BlockSpec((tc, D_MODEL), row), pl.BlockSpec((tc, LANES), row),
                  pl.BlockSpec((tc, LANES), row), pl.BlockSpec(memory_space=pl.ANY)],
        out_specs=pl.BlockSpec((tc, D_MODEL), row),
        scratch_shapes=[pltpu.VMEM((tc, D_MODEL), f32), pltpu.VMEM((tc, D_MODEL), f32), pltpu.SemaphoreType.DMA(())])
    return pl.pallas_call(
        _combine_kernel, grid_spec=grid_spec, out_shape=jax.ShapeDtypeStruct((t, D_MODEL), f32),
        compiler_params=_params(1), name="moe_combine",
    )(starts_pad, e_flat, r_flat, x2d, g1, g2, yb)


def _moe(x2d, norm_g, wg, bg, we, be, w_gate_b, w_up_b, w_down_b):
    t = x2d.shape[0]
    h, idx, g1, g2, cnt = _router(x2d, norm_g, wg, bg, we, be)
    counts = cnt[0, :N_EXPERTS].astype(jnp.int32)
    padded = ((counts + MOE_BLOCK - 1) // MOE_BLOCK) * MOE_BLOCK
    ends_pad = jnp.cumsum(padded)
    starts_pad = (ends_pad - padded).astype(jnp.int32)
    a = t * TOP_K
    p_rows = (-(-a // MOE_BLOCK)) * MOE_BLOCK + N_EXPERTS * MOE_BLOCK
    n_blocks = p_rows // MOE_BLOCK
    blk_start = jnp.arange(n_blocks, dtype=jnp.int32) * MOE_BLOCK
    blk_e = jnp.minimum(jnp.sum((ends_pad[None, :] <= blk_start[:, None]).astype(jnp.int32), axis=1), N_EXPERTS - 1)
    n_used = (ends_pad[-1:] // MOE_BLOCK).astype(jnp.int32)
    e_flat = idx[:, 0:TOP_K].reshape(a)
    r_flat = idx[:, TOP_K:2 * TOP_K].reshape(a)
    buf = _dispatch(h, e_flat, r_flat, starts_pad, p_rows)
    yb = _experts(buf, blk_e, n_used, w_gate_b, w_up_b, w_down_b)
    return _combine(x2d, g1, g2, yb, e_flat, r_flat, starts_pad)


def _split_w_in(w_in):
    sizes = (GROUP_WIDTH, SSD_CONV_CH, SSD_HEADS) + (GROUP_WIDTH,) * 8
    names = ("z", "xbc", "dt", "sc_b", "sc_c", "sc_h", "q", "k", "v", "cf_a", "cf_g")
    parts, o = {}, 0
    for nme, s in zip(names, sizes):
        parts[nme] = w_in[:, o:o + s]
        o += s
    parts["dt"] = jnp.pad(parts["dt"], ((0, 0), (0, LANES - SSD_HEADS)))
    return [parts[nme].astype(bf16) for nme, _ in _SEGS]


def kernel(x, mem, norm_mix, w_in, ssd_conv_w, ssd_conv_b, ssd_dt_bias, ssd_a_log, ssd_d, ssd_norm, sc_conv_w,
           attn_q_norm, attn_k_norm, cf_conv_w, cf_conv_b, cf_ln_g, cf_ln_b, w_out, norm_ca, norm_mem, ca_wq, ca_wkv,
           ca_q_norm, ca_k_norm, ca_wo, norm_ffn, router_group_w, router_group_b, router_expert_w, router_expert_b,
           exp_w_gate, exp_w_up, exp_w_down):
    bsz, seq, d = x.shape
    depth = w_in.shape[0]
    assert d == D_MODEL and seq % ATT_TILE == 0 and seq % TS_SSD == 0 and (bsz * seq) % TM_PROJ == 0
    x2d = x.reshape(bsz * seq, d)
    for l in range(depth):
        proj = _in_projection(x2d, norm_mix[l].reshape(1, -1), _split_w_in(w_in[l]))
        y_ssd = _ssd(proj, bsz, seq, ssd_conv_w[l], ssd_conv_b[l], ssd_dt_bias[l], ssd_a_log[l], ssd_d[l], ssd_norm[l])
        y_sc, y_cf = _convs(proj, bsz, seq, sc_conv_w[l], cf_conv_w[l], cf_conv_b[l], cf_ln_g[l], cf_ln_b[l])
        y_att = _dilated_attention(proj, bsz, seq, attn_q_norm[l], attn_k_norm[l])
        kn, vv = _memory_kv(mem, norm_mem[l], ca_wkv[l].astype(bf16), ca_k_norm[l])
        x2d = _outproj_cross_attention(x2d, (y_ssd, y_sc, y_att, y_cf), w_out[l].astype(bf16), bsz, seq, norm_ca[l],
                                       ca_wq[l].astype(bf16), kn, vv, ca_q_norm[l], ca_wo[l].astype(bf16))
        x2d = _moe(x2d, norm_ffn[l], router_group_w[l], router_group_b[l], router_expert_w[l], router_expert_b[l],
                   exp_w_gate[l], exp_w_up[l], exp_w_down[l])
    return x2d.reshape(bsz, seq, d)
```

```python
import functools

import numpy as np
import jax
import jax.numpy as jnp
from jax import lax
from jax.experimental import pallas as pl
from jax.experimental.pallas import tpu as pltpu

f32 = jnp.float32
bf16 = jnp.bfloat16

D_MODEL = 1024
GROUP_WIDTH = 384
SSD_HEADS = 6
SSD_HEAD_DIM = 64
SSD_GROUPS = 2
SSD_STATE = 128
SSD_CONV = 4
SSD_CHUNK = 128
SSD_CONV_CH = GROUP_WIDTH + 2 * SSD_GROUPS * SSD_STATE
SC_WIDTH = 3
ATT_HEADS = 6
ATT_HEAD_DIM = 64
ATT_BLOCK = 128
DILATIONS = (1, 4, 16)
ATT_TILE = ATT_BLOCK * max(DILATIONS)
ATT_GROUP = 4
CF_KERNEL = 31
CA_HEADS = 4
CA_HEAD_DIM = D_MODEL // CA_HEADS
N_EXPERT_GROUPS = 4
EXPERTS_PER_GROUP = 8
N_EXPERTS = N_EXPERT_GROUPS * EXPERTS_PER_GROUP
TOP_K = 2
EXPERT_HIDDEN = D_MODEL // 2
MOE_BLOCK = 256
RMS_EPS = 1e-6
LN_EPS = 1e-5

LANES = 128
SUBLANES = 8
VMEM_LIMIT_BYTES = 56 * 1024 * 1024

TM_PROJ = 512
TS_SSD = 1024
TS_CONV = 512
CONV_ROWS = 64
TM_CA = 512
TM_ROUTER = 512
TD_DISPATCH = 1024
TC_COMBINE = 1024
TOKEN_TILE_ROWS = 8
DMA_UNROLL = 8

_SEGS = (("xbc", SSD_CONV_CH), ("dt", LANES), ("z", GROUP_WIDTH), ("sc_b", GROUP_WIDTH), ("sc_c", GROUP_WIDTH),
         ("sc_h", GROUP_WIDTH), ("q", GROUP_WIDTH), ("k", GROUP_WIDTH), ("v", GROUP_WIDTH), ("cf_a", GROUP_WIDTH),
         ("cf_g", GROUP_WIDTH))


def _params(n_axes):
    return pltpu.CompilerParams(dimension_semantics=("arbitrary",) * n_axes, vmem_limit_bytes=VMEM_LIMIT_BYTES)


def _dot(a, b):
    return jnp.dot(a, b, preferred_element_type=f32)


def _dot_nt(a, b):
    return lax.dot_general(a, b, (((1,), (1,)), ((), ())), preferred_element_type=f32)


def _split2(x):
    hi = x.astype(bf16)
    lo = (x - hi.astype(f32)).astype(bf16)
    return hi, lo


def _split3(x):
    hi = x.astype(bf16)
    r = x - hi.astype(f32)
    mid = r.astype(bf16)
    lo = (r - mid.astype(f32)).astype(bf16)
    return hi, mid, lo


def _dot_exact_rhs(x, m_bf16, n_split):
    parts = _split3(x) if n_split == 3 else _split2(x)
    acc = _dot(parts[0], m_bf16)
    for p in parts[1:]:
        acc = acc + _dot(p, m_bf16)
    return acc


def _sigmoid(x):
    return 1.0 / (1.0 + jnp.exp(-x))


def _silu(x):
    return x * _sigmoid(x)


def _rms(x, gain):
    ms = jnp.mean(x * x, axis=-1, keepdims=True)
    return x * lax.rsqrt(ms + RMS_EPS) * gain


def _inproj_kernel(x_ref, g_ref, *refs):
    n = len(_SEGS)
    w_refs, o_refs = refs[:n], refs[n:]
    h = _rms(x_ref[...], g_ref[...]).astype(bf16)
    for w_ref, o_ref in zip(w_refs, o_refs):
        o_ref[...] = _dot(h, w_ref[...])


def _in_projection(x2d, gain, w_segs):
    t = x2d.shape[0]
    tm = TM_PROJ
    in_specs = [pl.BlockSpec((tm, D_MODEL), lambda i: (i, 0)), pl.BlockSpec((1, D_MODEL), lambda i: (0, 0))]
    in_specs += [pl.BlockSpec((D_MODEL, w), lambda i: (0, 0)) for _, w in _SEGS]
    out_specs = [pl.BlockSpec((tm, w), lambda i: (i, 0)) for _, w in _SEGS]
    out_shape = [jax.ShapeDtypeStruct((t, w), f32) for _, w in _SEGS]
    outs = pl.pallas_call(
        _inproj_kernel, grid=(t // tm,), in_specs=in_specs, out_specs=out_specs, out_shape=out_shape,
        compiler_params=_params(1), name="in_projection",
    )(x2d, gain, *w_segs)
    return dict(zip([n for n, _ in _SEGS], outs))


def _ssd_kernel(xbc_ref, dt_ref, z_ref, cw_ref, cb_ref, dtb_ref, alog_ref, dvec_ref, ng_ref, tril_ref, e384_ref,
                e768_ref, o_ref, xbuf, xc, state):
    ts = TS_SSD
    q = SSD_CHUNK
    halo = SUBLANES

    @pl.when(pl.program_id(1) == 0)
    def _():
        xbuf[0:halo, :] = jnp.zeros((halo, SSD_CONV_CH), f32)
        state[...] = jnp.zeros_like(state)

    xbuf[halo:halo + ts, :] = xbc_ref[...]
    acc = cb_ref[...] + cw_ref[0:1, :] * xbuf[halo - 3:halo - 3 + ts, :]
    for k in range(1, SSD_CONV):
        acc = acc + cw_ref[k:k + 1, :] * xbuf[halo - 3 + k:halo - 3 + k + ts, :]
    xc[...] = _silu(acc)
    xbuf[0:halo, :] = xbuf[ts:ts + halo, :]

    lane_row = lax.broadcasted_iota(jnp.int32, (1, LANES), 1)
    head_lane = lane_row < SSD_HEADS
    a_row = jnp.where(head_lane, -jnp.exp(alog_ref[...]), 0.0)
    lane_sq = lax.broadcasted_iota(jnp.int32, (q, LANES), 1)
    row_sq = lax.broadcasted_iota(jnp.int32, (q, LANES), 0)
    causal = row_sq >= lane_sq
    left = lane_sq < SSD_HEAD_DIM
    tril = tril_ref[...]
    e384 = e384_ref[...]
    e768 = e768_ref[...]
    dvec = dvec_ref[...]
    ng = ng_ref[...]

    def chunk(c, carry):
        r0 = pl.multiple_of(c * q, q)
        dtr = dt_ref[pl.ds(r0, q), :] + dtb_ref[...]
        sp = jnp.maximum(dtr, 0.0) + jnp.log1p(jnp.exp(-jnp.abs(dtr)))
        dt = jnp.where(head_lane, sp, 0.0)
        da = dt * a_row
        d3 = _split3(da)
        cs = _dot(tril, d3[0]) + _dot(tril, d3[1]) + _dot(tril, d3[2])
        cs_t = cs.T
        c3 = _split3(cs)
        cs384 = _dot(c3[0], e384) + _dot(c3[1], e384) + _dot(c3[2], e384)
        cs768 = _dot(c3[0], e768) + _dot(c3[1], e768) + _dot(c3[2], e768)
        dt384 = _dot_exact_rhs(dt, e384, 2)
        last384 = cs384[q - 1:q, :]
        decay384 = jnp.exp(last384 - cs384)
        expcs384 = jnp.exp(cs384)

        xs = xc[pl.ds(r0, q), 0:GROUP_WIDTH]
        bm = xc[pl.ds(r0, q), GROUP_WIDTH:GROUP_WIDTH + SSD_GROUPS * SSD_STATE]
        cm = xc[pl.ds(r0, q), GROUP_WIDTH + SSD_GROUPS * SSD_STATE:SSD_CONV_CH]
        xdt = xs * dt384
        xdt_b = xdt.astype(bf16)
        xdec_b = (xdt * decay384).astype(bf16)
        st = state[...]
        st_b = st.astype(bf16)

        cb_g, bt_g, c_g = [], [], []
        for g in range(SSD_GROUPS):
            b_f = bm[:, g * SSD_STATE:(g + 1) * SSD_STATE]
            c_b = cm[:, g * SSD_STATE:(g + 1) * SSD_STATE].astype(bf16)
            cb_g.append(_dot_nt(c_b, b_f.astype(bf16)))
            bt_g.append(b_f.T.astype(bf16))
            c_g.append(c_b)

        y_parts, st_parts = [], []
        for p in range(SSD_HEADS // 2):
            sl = slice(p * LANES, (p + 1) * LANES)
            xdt_p, xdec_p, st_p = xdt_b[:, sl], xdec_b[:, sl], st_b[:, sl]
            yd, s_new, yo = [], {}, {}
            for hh in range(2):
                h = 2 * p + hh
                g = h // (SSD_HEADS // SSD_GROUPS)
                diff = cs768[:, h * LANES:(h + 1) * LANES] - cs_t[h:h + 1, :]
                seg = jnp.exp(jnp.where(causal, diff, -1e30))
                m = (cb_g[g] * seg).astype(bf16)
                yd.append(_dot(m, xdt_p))
                if g not in s_new:
                    s_new[g] = _dot(bt_g[g], xdec_p)
                    yo[g] = _dot(c_g[g], st_p)
            g0 = (2 * p) // (SSD_HEADS // SSD_GROUPS)
            g1 = (2 * p + 1) // (SSD_HEADS // SSD_GROUPS)
            y_diag = jnp.where(left, yd[0], yd[1])
            y_off = jnp.where(left, yo[g0], yo[g1])
            s_pair = jnp.where(left, s_new[g0], s_new[g1])
            y_parts.append(y_diag + y_off * expcs384[:, sl] + xs[:, sl] * dvec[:, sl])
            st_parts.append(st[:, sl] * expcs384[q - 1:q, sl] + s_pair)
        state[...] = jnp.concatenate(st_parts, axis=-1)
        y = jnp.concatenate(y_parts, axis=-1)
        yg = y * _silu(z_ref[pl.ds(r0, q), :])
        o_ref[pl.ds(r0, q), :] = _rms(yg, ng)
        return carry

    lax.fori_loop(0, ts // q, chunk, 0)


def _ssd_constants():
    q = SSD_CHUNK
    tril = np.tril(np.ones((q, q), np.float32))
    e384 = np.zeros((LANES, GROUP_WIDTH), np.float32)
    e768 = np.zeros((LANES, SSD_HEADS * LANES), np.float32)
    for h in range(SSD_HEADS):
        e384[h, h * SSD_HEAD_DIM:(h + 1) * SSD_HEAD_DIM] = 1.0
        e768[h, h * LANES:(h + 1) * LANES] = 1.0
    return jnp.asarray(tril, bf16), jnp.asarray(e384, bf16), jnp.asarray(e768, bf16)


def _ssd(proj, bsz, seq, conv_w, conv_b, dt_bias, a_log, d_skip, norm_g):
    t = bsz * seq
    ts = TS_SSD
    ns = seq // ts
    tril, e384, e768 = _ssd_constants()
    pad = LANES - SSD_HEADS
    dtb = jnp.pad(dt_bias, (0, pad)).reshape(1, LANES)
    alog = jnp.pad(a_log, (0, pad)).reshape(1, LANES)
    dvec = jnp.repeat(d_skip, SSD_HEAD_DIM).reshape(1, GROUP_WIDTH)
    row = lambda b, i: (b * ns + i, 0)
    const = lambda b, i: (0, 0)
    return pl.pallas_call(
        _ssd_kernel, grid=(bsz, ns),
        in_specs=[pl.BlockSpec((ts, SSD_CONV_CH), row), pl.BlockSpec((ts, LANES), row), pl.BlockSpec((ts, GROUP_WIDTH), row),
                  pl.BlockSpec((SSD_CONV, SSD_CONV_CH), const), pl.BlockSpec((1, SSD_CONV_CH), const),
                  pl.BlockSpec((1, LANES), const), pl.BlockSpec((1, LANES), const), pl.BlockSpec((1, GROUP_WIDTH), const),
                  pl.BlockSpec((1, GROUP_WIDTH), const), pl.BlockSpec((SSD_CHUNK, SSD_CHUNK), const),
                  pl.BlockSpec((LANES, GROUP_WIDTH), const), pl.BlockSpec((LANES, SSD_HEADS * LANES), const)],
        out_specs=pl.BlockSpec((ts, GROUP_WIDTH), row),
        out_shape=jax.ShapeDtypeStruct((t, GROUP_WIDTH), f32),
        scratch_shapes=[pltpu.VMEM((ts + SUBLANES, SSD_CONV_CH), f32), pltpu.VMEM((ts, SSD_CONV_CH), f32),
                        pltpu.VMEM((SSD_STATE, GROUP_WIDTH), f32)],
        compiler_params=_params(2), name="ssd_scan",
    )(proj["xbc"], proj["dt"], proj["z"], conv_w, conv_b.reshape(1, -1), dtb, alog, dvec, norm_g.reshape(1, -1), tril,
      e384, e768)


_CF_HALO = 32


def _conv_kernel(scb_ref, scc_ref, sch_ref, cfa_ref, cfg_ref, scw_ref, cfw_ref, cfb_ref, lng_ref, lnb_ref, ysc_ref,
                 ycf_ref, ubuf, gbuf):
    ts = TS_CONV
    uh = SUBLANES

    @pl.when(pl.program_id(1) == 0)
    def _():
        ubuf[0:uh, :] = jnp.zeros((uh, GROUP_WIDTH), f32)
        gbuf[0:_CF_HALO, :] = jnp.zeros((_CF_HALO, GROUP_WIDTH), f32)

    ubuf[uh:uh + ts, :] = scc_ref[...] * sch_ref[...]
    gbuf[_CF_HALO:_CF_HALO + ts, :] = cfa_ref[...] * _sigmoid(cfg_ref[...])

    o = uh - (SC_WIDTH - 1)
    acc = scw_ref[0:1, :] * ubuf[o:o + ts, :]
    for k in range(1, SC_WIDTH):
        acc = acc + scw_ref[k:k + 1, :] * ubuf[o + k:o + k + ts, :]
    ysc_ref[...] = scb_ref[...] * acc

    o = _CF_HALO - (CF_KERNEL - 1)
    cfb = cfb_ref[...]
    lng = lng_ref[...]
    lnb = lnb_ref[...]
    for c in range(ts // CONV_ROWS):
        r0 = c * CONV_ROWS
        acc = cfb
        for s in range(SUBLANES):
            rows = CONV_ROWS + (SUBLANES if s else 0)
            part = None
            for k in range(CF_KERNEL):
                if (o + k) % SUBLANES != s:
                    continue
                a = r0 + o + k - s
                term = cfw_ref[k:k + 1, :] * gbuf[a:a + rows, :]
                part = term if part is None else part + term
            acc = acc + part[s:s + CONV_ROWS, :]
        mu = jnp.mean(acc, axis=-1, keepdims=True)
        xc = acc - mu
        var = jnp.mean(xc * xc, axis=-1, keepdims=True)
        y = xc * lax.rsqrt(var + LN_EPS) * lng + lnb
        ycf_ref[r0:r0 + CONV_ROWS, :] = _silu(y)

    ubuf[0:uh, :] = ubuf[ts:ts + uh, :]
    gbuf[0:_CF_HALO, :] = gbuf[ts:ts + _CF_HALO, :]


def _convs(proj, bsz, seq, sc_w, cf_w, cf_b, ln_g, ln_b):
    t = bsz * seq
    ts = TS_CONV
    ns = seq // ts
    row = lambda b, i: (b * ns + i, 0)
    const = lambda b, i: (0, 0)
    act = pl.BlockSpec((ts, GROUP_WIDTH), row)
    vec = pl.BlockSpec((1, GROUP_WIDTH), const)
    return pl.pallas_call(
        _conv_kernel, grid=(bsz, ns),
        in_specs=[act, act, act, act, act, pl.BlockSpec((SC_WIDTH, GROUP_WIDTH), const),
                  pl.BlockSpec((CF_KERNEL, GROUP_WIDTH), const), vec, vec, vec],
        out_specs=[act, act],
        out_shape=[jax.ShapeDtypeStruct((t, GROUP_WIDTH), f32)] * 2,
        scratch_shapes=[pltpu.VMEM((ts + SUBLANES, GROUP_WIDTH), f32), pltpu.VMEM((ts + _CF_HALO, GROUP_WIDTH), f32)],
        compiler_params=_params(2), name="gated_convs",
    )(proj["sc_b"], proj["sc_c"], proj["sc_h"], proj["cf_a"], proj["cf_g"], sc_w, cf_w, cf_b.reshape(1, -1),
      ln_g.reshape(1, -1), ln_b.reshape(1, -1))


def _attn_kernel(q_ref, k_ref, v_ref, qg_ref, kg_ref, bsum_ref, o_ref, qbuf, kbuf, vbuf, acc_o, acc_m, acc_l):
    ta = ATT_TILE
    n = ATT_BLOCK
    tile = pl.program_id(2)

    @pl.when(tile == 0)
    def _():
        kbuf[0:ta, :] = jnp.zeros((ta, LANES), f32)
        vbuf[0:ta, :] = jnp.zeros((ta, LANES), f32)

    bsum = bsum_ref[...]

    def head_rms(x, g):
        ms = _dot_exact_rhs(x * x, bsum, 2)
        return x * lax.rsqrt(ms + RMS_EPS) * g

    qbuf[...] = head_rms(q_ref[...], qg_ref[...]) * (ATT_HEAD_DIM ** -0.5)
    kbuf[ta:2 * ta, :] = head_rms(k_ref[...], kg_ref[...])
    vbuf[ta:2 * ta, :] = v_ref[...]

    qq = lax.broadcasted_iota(jnp.int32, (n, 2 * n), 0)
    kk = lax.broadcasted_iota(jnp.int32, (n, 2 * n), 1)
    rel = qq + n - kk
    band = (rel >= 0) & (rel <= n)
    cur = kk >= n
    lane = lax.broadcasted_iota(jnp.int32, (n, LANES), 1)
    left = lane < ATT_HEAD_DIM
    ones = jnp.ones((2 * n, LANES), bf16)

    for d in DILATIONS:
        shift = d.bit_length() - 1

        def rows(start, size, d=d):
            return pl.ds(start, size) if d == 1 else pl.ds(start, size, stride=d)

        def body(it, carry, d=d, shift=shift, rows=rows):
            loaded = []
            for g in range(ATT_GROUP):
                idx = it * ATT_GROUP + g
                nb = idx >> shift
                r = idx & (d - 1)
                qs = nb * (n * d) + r
                ks = ta + qs - n * d
                q_r = qbuf[rows(qs, n), :]
                k_r = kbuf[rows(ks, 2 * n), :].astype(bf16)
                v_r = jnp.concatenate([vbuf[rows(ks, 2 * n), :].astype(bf16), ones], axis=-1)
                old = None
                if d != 1:
                    old = (acc_o[rows(qs, n), :], acc_m[rows(qs, n), :], acc_l[rows(qs, n), :])
                valid = band & (cur | ((tile > 0) | (nb > 0)))
                loaded.append((qs, q_r, k_r, v_r, old, valid))
            merged = []
            for qs, q_r, k_r, v_r, old, valid in loaded:
                o_h, m_h, l_h = [], [], []
                for hh in range(2):
                    qm = jnp.where(left if hh == 0 else ~left, q_r, 0.0).astype(bf16)
                    s = jnp.where(valid, _dot_nt(qm, k_r), -1e30)
                    m = jnp.max(s, axis=-1, keepdims=True)
                    pv = _dot(jnp.exp(s - m).astype(bf16), v_r)
                    m_h.append(m)
                    o_h.append(pv[:, :LANES])
                    l_h.append(pv[:, LANES:])
                o_new = jnp.where(left, o_h[0], o_h[1])
                m_new = jnp.where(left, m_h[0], m_h[1])
                l_new = jnp.where(left, l_h[0], l_h[1])
                if old is not None:
                    o_old, m_old, l_old = old
                    m_tot = jnp.maximum(m_old, m_new)
                    a = jnp.exp(m_old - m_tot)
                    b = jnp.exp(m_new - m_tot)
                    o_new = a * o_old + b * o_new
                    l_new = a * l_old + b * l_new
                    m_new = m_tot
                merged.append((qs, o_new, m_new, l_new))
            for qs, o_new, m_new, l_new in merged:
                acc_o[rows(qs, n), :] = o_new
                acc_m[rows(qs, n), :] = jnp.broadcast_to(m_new, (n, LANES))
                acc_l[rows(qs, n), :] = jnp.broadcast_to(l_new, (n, LANES))
            return carry

        lax.fori_loop(0, ta // n // ATT_GROUP, body, 0)

    o_ref[...] = acc_o[...] / acc_l[...]
    kbuf[0:ta, :] = kbuf[ta:2 * ta, :]
    vbuf[0:ta, :] = vbuf[ta:2 * ta, :]


def _dilated_attention(proj, bsz, seq, q_norm, k_norm):
    t = bsz * seq
    ta = ATT_TILE
    nt = seq // ta
    pairs = ATT_HEADS // 2
    bsum = np.zeros((LANES, LANES), np.float32)
    for h in range(2):
        bsum[h * ATT_HEAD_DIM:(h + 1) * ATT_HEAD_DIM, h * ATT_HEAD_DIM:(h + 1) * ATT_HEAD_DIM] = 1.0 / ATT_HEAD_DIM
    blk = pl.BlockSpec((ta, LANES), lambda b, p, i: (b * nt + i, p))
    const = lambda b, p, i: (0, 0)
    vec = pl.BlockSpec((1, LANES), const)
    buf = lambda rows: pltpu.VMEM((rows, LANES), f32)
    return pl.pallas_call(
        _attn_kernel, grid=(bsz, pairs, nt),
        in_specs=[blk, blk, blk, vec, vec, pl.BlockSpec((LANES, LANES), const)],
        out_specs=blk,
        out_shape=jax.ShapeDtypeStruct((t, GROUP_WIDTH), f32),
        scratch_shapes=[buf(ta), buf(2 * ta), buf(2 * ta), buf(ta), buf(ta), buf(ta)],
        compiler_params=_params(3), name="dilated_attention",
    )(proj["q"], proj["k"], proj["v"], jnp.tile(q_norm, 2).reshape(1, LANES), jnp.tile(k_norm, 2).reshape(1, LANES),
      jnp.asarray(bsum, bf16))


def _memkv_kernel(mem_ref, gm_ref, wkv_ref, gk_ref, k_ref, v_ref):
    hm = _rms(mem_ref[...], gm_ref[...]).astype(bf16)
    kv = _dot(hm, wkv_ref[...])
    gk = gk_ref[...]
    for hh in range(CA_HEADS):
        sl = slice(hh * CA_HEAD_DIM, (hh + 1) * CA_HEAD_DIM)
        k_ref[:, sl] = _rms(kv[:, sl], gk).astype(bf16)
    v_ref[...] = kv[:, D_MODEL:].astype(bf16)


def _memory_kv(mem, norm_mem, wkv_b, k_norm):
    bsz, m, _ = mem.shape
    const = lambda b: (0, 0)
    blk = pl.BlockSpec((None, m, D_MODEL), lambda b: (b, 0, 0))
    return pl.pallas_call(
        _memkv_kernel, grid=(bsz,),
        in_specs=[blk, pl.BlockSpec((1, D_MODEL), const), pl.BlockSpec((D_MODEL, 2 * D_MODEL), const),
                  pl.BlockSpec((1, CA_HEAD_DIM), const)],
        out_specs=[blk, blk],
        out_shape=[jax.ShapeDtypeStruct((bsz, m, D_MODEL), bf16)] * 2,
        compiler_params=_params(1), name="memory_kv",
    )(mem, norm_mem.reshape(1, -1), wkv_b, k_norm.reshape(1, -1))


def _outca_kernel(x_ref, y0_ref, y1_ref, y2_ref, y3_ref, w0_ref, w1_ref, w2_ref, w3_ref, gca_ref, wq_ref, kn_ref,
                  v_ref, gq_ref, wo_ref, o_ref):
    x1 = x_ref[...]
    for y_ref, w_ref in ((y0_ref, w0_ref), (y1_ref, w1_ref), (y2_ref, w2_ref), (y3_ref, w3_ref)):
        x1 = x1 + _dot(y_ref[...].astype(bf16), w_ref[...])
    h = _rms(x1, gca_ref[...]).astype(bf16)
    q = _dot(h, wq_ref[...])
    gq = gq_ref[...]
    outs = []
    for hh in range(CA_HEADS):
        sl = slice(hh * CA_HEAD_DIM, (hh + 1) * CA_HEAD_DIM)
        qn = (_rms(q[:, sl], gq) * (CA_HEAD_DIM ** -0.5)).astype(bf16)
        s = _dot_nt(qn, kn_ref[:, sl])
        m = jnp.max(s, axis=-1, keepdims=True)
        p = jnp.exp(s - m)
        l = jnp.sum(p, axis=-1, keepdims=True)
        outs.append((_dot(p.astype(bf16), v_ref[:, sl]) / l).astype(bf16))
    o_ref[...] = x1 + _dot(jnp.concatenate(outs, axis=-1), wo_ref[...])


def _outproj_cross_attention(x2d, ys, w_out_b, bsz, seq, norm_ca, wq_b, kn, vv, q_norm, wo_b):
    t = bsz * seq
    tm = TM_CA
    ns = seq // tm
    m = kn.shape[1]
    row = lambda b, i: (b * ns + i, 0)
    const = lambda b, i: (0, 0)
    mix = pl.BlockSpec((tm, GROUP_WIDTH), row)
    wblk = pl.BlockSpec((GROUP_WIDTH, D_MODEL), const)
    sq = pl.BlockSpec((D_MODEL, D_MODEL), const)
    kvb = pl.BlockSpec((None, m, D_MODEL), lambda b, i: (b, 0, 0))
    w_parts = [w_out_b[j * GROUP_WIDTH:(j + 1) * GROUP_WIDTH] for j in range(4)]
    return pl.pallas_call(
        _outca_kernel, grid=(bsz, ns),
        in_specs=[pl.BlockSpec((tm, D_MODEL), row), mix, mix, mix, mix, wblk, wblk, wblk, wblk,
                  pl.BlockSpec((1, D_MODEL), const), sq, kvb, kvb, pl.BlockSpec((1, CA_HEAD_DIM), const), sq],
        out_specs=pl.BlockSpec((tm, D_MODEL), row),
        out_shape=jax.ShapeDtypeStruct((t, D_MODEL), f32),
        compiler_params=_params(2), name="outproj_cross_attention",
    )(x2d, *ys, *w_parts, norm_ca.reshape(1, -1), wq_b, kn, vv, q_norm.reshape(1, -1), wo_b)


def _to_token_tiles(ref, x):
    rows = x.shape[0]
    for s in range(TOKEN_TILE_ROWS):
        ref[pl.ds(s, rows, stride=TOKEN_TILE_ROWS), :] = x[:, s * LANES:(s + 1) * LANES]


def _from_token_tiles(ref, rows):
    return jnp.concatenate([ref[pl.ds(s, rows, stride=TOKEN_TILE_ROWS), :] for s in range(TOKEN_TILE_ROWS)], axis=-1)


def _router_kernel(x_ref, g_ref, w1_ref, w2_ref, b_ref, slt_ref, h_ref, meta_ref, g1_ref, g2_ref, cnt_ref, base):
    @pl.when(pl.program_id(0) == 0)
    def _():
        base[...] = jnp.zeros_like(base)

    tm = TM_ROUTER
    h = _rms(x_ref[...], g_ref[...])
    _to_token_tiles(h_ref, h)
    h1, h2 = _split2(h)
    w1 = w1_ref[...]
    logits = (_dot(h1, w1) + _dot(h1, w2_ref[...]) + _dot(h2, w1)) + b_ref[...]
    lane = lax.broadcasted_iota(jnp.int32, (tm, LANES), 1)
    lanef = lane.astype(f32)
    neg = -jnp.inf

    lgm = jnp.where((lane >= N_EXPERTS) & (lane < N_EXPERTS + N_EXPERT_GROUPS), logits, neg)
    mg = jnp.max(lgm, axis=-1, keepdims=True)
    g_p = 1.0 / jnp.sum(jnp.exp(lgm - mg), axis=-1, keepdims=True)
    gi = jnp.min(jnp.where(lgm == mg, lanef, float(LANES)), axis=-1, keepdims=True) - float(N_EXPERTS)

    grp = (lane >> (EXPERTS_PER_GROUP.bit_length() - 1)).astype(f32)
    lem = jnp.where((grp == gi) & (lane < N_EXPERTS), logits, neg)
    me = jnp.max(lem, axis=-1, keepdims=True)
    i1 = jnp.min(jnp.where(lem == me, lanef, float(LANES)), axis=-1, keepdims=True)
    le2 = jnp.where(lanef == i1, neg, lem)
    m2 = jnp.max(le2, axis=-1, keepdims=True)
    i2 = jnp.min(jnp.where(le2 == m2, lanef, float(LANES)), axis=-1, keepdims=True)
    e2 = jnp.exp(m2 - me)
    gate1 = g_p / (1.0 + e2)
    gate2 = g_p * e2 / (1.0 + e2)

    oh1 = (lanef == i1).astype(f32)
    oh2 = (lanef == i2).astype(f32)
    cnt = oh1 + oh2
    before = _dot(slt_ref[...], cnt.astype(bf16)) + base[...]
    r1 = jnp.sum(oh1 * before, axis=-1, keepdims=True)
    r2 = jnp.sum(oh2 * before, axis=-1, keepdims=True)
    total = base[...] + jnp.sum(cnt, axis=0, keepdims=True)
    base[...] = total
    cnt_ref[...] = total

    packed = jnp.where(lane == 0, i1, jnp.where(lane == 1, i2, jnp.where(lane == 2, r1, r2)))
    meta_ref[...] = packed.T[0:SUBLANES, :].astype(jnp.int32)
    g1_ref[...] = jnp.broadcast_to(gate1, (tm, LANES))
    g2_ref[...] = jnp.broadcast_to(gate2, (tm, LANES))


def _router(x2d, norm_g, wg, bg, we, be):
    t = x2d.shape[0]
    tm = TM_ROUTER
    pad = LANES - N_EXPERTS - N_EXPERT_GROUPS
    wr = jnp.pad(jnp.concatenate([we, wg], axis=1), ((0, 0), (0, pad)))
    br = jnp.pad(jnp.concatenate([be, bg]), (0, pad)).reshape(1, LANES)
    w1, w2 = _split2(wr)
    slt = jnp.asarray(np.tril(np.ones((tm, tm), np.float32), -1), bf16)
    const = lambda i: (0, 0)
    row = lambda i: (i, 0)
    wspec = pl.BlockSpec((D_MODEL, LANES), const)
    gate = pl.BlockSpec((tm, LANES), row)
    return pl.pallas_call(
        _router_kernel, grid=(t // tm,),
        in_specs=[pl.BlockSpec((tm, D_MODEL), row), pl.BlockSpec((1, D_MODEL), const), wspec, wspec,
                  pl.BlockSpec((1, LANES), const), pl.BlockSpec((tm, tm), const)],
        out_specs=[pl.BlockSpec((tm * TOKEN_TILE_ROWS, LANES), row), pl.BlockSpec((SUBLANES, tm), lambda i: (0, i)),
                   gate, gate, pl.BlockSpec((1, LANES), const)],
        out_shape=[jax.ShapeDtypeStruct((t * TOKEN_TILE_ROWS, LANES), f32), jax.ShapeDtypeStruct((SUBLANES, t), jnp.int32),
                   jax.ShapeDtypeStruct((t, LANES), f32), jax.ShapeDtypeStruct((t, LANES), f32),
                   jax.ShapeDtypeStruct((1, LANES), f32)],
        scratch_shapes=[pltpu.VMEM((1, LANES), f32)],
        compiler_params=_params(1), name="moe_router",
    )(x2d, norm_g.reshape(1, -1), w1, w2, br, slt)


def _tile_copy(src, src_tile, dst, dst_tile, sem):
    rows = TOKEN_TILE_ROWS
    return pltpu.make_async_copy(src.at[pl.ds(pl.multiple_of(src_tile * rows, rows), rows)],
                                 dst.at[pl.ds(pl.multiple_of(dst_tile * rows, rows), rows)], sem)


def _dispatch_kernel(starts_ref, e1_ref, e2_ref, r1_ref, r2_ref, h_ref, zero_ref, buf_ref, sem):
    del zero_ref

    def issue(j, carry):
        _tile_copy(h_ref, j, buf_ref, starts_ref[e1_ref[j]] + r1_ref[j], sem).start()
        _tile_copy(h_ref, j, buf_ref, starts_ref[e2_ref[j]] + r2_ref[j], sem).start()
        return carry

    lax.fori_loop(0, TD_DISPATCH, issue, 0, unroll=DMA_UNROLL)

    def drain(j, carry):
        _tile_copy(h_ref, 0, buf_ref, 0, sem).wait()
        _tile_copy(h_ref, 0, buf_ref, 0, sem).wait()
        return carry

    lax.fori_loop(0, TD_DISPATCH, drain, 0, unroll=DMA_UNROLL)


def _dispatch(h_tiles, meta, starts_pad, p_rows):
    t = meta.shape[1]
    td = TD_DISPATCH
    smem = pl.BlockSpec((td,), lambda i, s: (i,), memory_space=pltpu.SMEM)
    anyspec = pl.BlockSpec(memory_space=pl.ANY)
    hspec = pl.BlockSpec((td * TOKEN_TILE_ROWS, LANES), lambda i, s: (i, 0))
    grid_spec = pltpu.PrefetchScalarGridSpec(
        num_scalar_prefetch=1, grid=(t // td,), in_specs=[smem, smem, smem, smem, hspec, anyspec], out_specs=anyspec,
        scratch_shapes=[pltpu.SemaphoreType.DMA(())])
    shape = (p_rows * TOKEN_TILE_ROWS, LANES)
    return pl.pallas_call(
        _dispatch_kernel, grid_spec=grid_spec, out_shape=jax.ShapeDtypeStruct(shape, f32),
        input_output_aliases={6: 0}, compiler_params=_params(1), name="moe_dispatch",
    )(starts_pad, meta[0], meta[1], meta[2], meta[3], h_tiles, jnp.zeros(shape, f32))


def _expert_kernel(blk_e_ref, nused_ref, x_ref, wg_ref, wu_ref, wd_ref, o_ref):
    del blk_e_ref
    used = pl.program_id(0) < nused_ref[0]

    @pl.when(used)
    def _():
        x = _from_token_tiles(x_ref, MOE_BLOCK).astype(bf16)
        g = _dot(x, wg_ref[...].astype(bf16))
        u = _dot(x, wu_ref[...].astype(bf16))
        _to_token_tiles(o_ref, _dot((_silu(g) * u).astype(bf16), wd_ref[...].astype(bf16)))

    @pl.when(jnp.logical_not(used))
    def _():
        o_ref[...] = jnp.zeros_like(o_ref)


def _experts(buf, blk_e, n_used, layer, w_gate, w_up, w_down):
    n_blocks = buf.shape[0] // (MOE_BLOCK * TOKEN_TILE_ROWS)
    xspec = pl.BlockSpec((MOE_BLOCK * TOKEN_TILE_ROWS, LANES), lambda i, be, nu: (i, 0))
    wspec = lambda a, b: pl.BlockSpec((None, None, a, b), lambda i, be, nu: (layer, be[i], 0, 0))
    grid_spec = pltpu.PrefetchScalarGridSpec(
        num_scalar_prefetch=2, grid=(n_blocks,),
        in_specs=[xspec, wspec(D_MODEL, EXPERT_HIDDEN), wspec(D_MODEL, EXPERT_HIDDEN), wspec(EXPERT_HIDDEN, D_MODEL)],
        out_specs=xspec)
    return pl.pallas_call(
        _expert_kernel, grid_spec=grid_spec, out_shape=jax.ShapeDtypeStruct(buf.shape, f32),
        compiler_params=_params(1), name="moe_experts",
    )(blk_e, n_used, buf, w_gate, w_up, w_down)


def _combine_kernel(starts_ref, e1_ref, e2_ref, r1_ref, r2_ref, x_ref, g1_ref, g2_ref, yb_ref, o_ref, rows0, rows1, sem):
    tc = TC_COMBINE

    def issue(j, carry):
        _tile_copy(yb_ref, starts_ref[e1_ref[j]] + r1_ref[j], rows0, j, sem).start()
        _tile_copy(yb_ref, starts_ref[e2_ref[j]] + r2_ref[j], rows1, j, sem).start()
        return carry

    lax.fori_loop(0, tc, issue, 0, unroll=DMA_UNROLL)

    def drain(j, carry):
        _tile_copy(yb_ref, 0, rows0, 0, sem).wait()
        _tile_copy(yb_ref, 0, rows1, 0, sem).wait()
        return carry

    lax.fori_loop(0, tc, drain, 0, unroll=DMA_UNROLL)
    reps = D_MODEL // LANES
    g1 = jnp.concatenate([g1_ref[...]] * reps, axis=-1)
    g2 = jnp.concatenate([g2_ref[...]] * reps, axis=-1)
    o_ref[...] = x_ref[...] + (g1 * _from_token_tiles(rows0, tc) + g2 * _from_token_tiles(rows1, tc))


def _combine(x2d, g1, g2, yb, meta, starts_pad):
    t = x2d.shape[0]
    tc = TC_COMBINE
    smem = pl.BlockSpec((tc,), lambda i, s: (i,), memory_space=pltpu.SMEM)
    row = lambda i, s: (i, 0)
    tiles = pltpu.VMEM((tc * TOKEN_TILE_ROWS, LANES), f32)
    grid_spec = pltpu.PrefetchScalarGridSpec(
        num_scalar_prefetch=1, grid=(t // tc,),
        in_specs=[smem, smem, smem, smem, pl.BlockSpec((tc, D_MODEL), row), pl.BlockSpec((tc, LANES), row),
                  pl.BlockSpec((tc, LANES), row), pl.BlockSpec(memory_space=pl.ANY)],
        out_specs=pl.BlockSpec((tc, D_MODEL), row),
        scratch_shapes=[tiles, tiles, pltpu.SemaphoreType.DMA(())])
    return pl.pallas_call(
        _combine_kernel, grid_spec=grid_spec, out_shape=jax.ShapeDtypeStruct((t, D_MODEL), f32),
        compiler_params=_params(1), name="moe_combine",
    )(starts_pad, meta[0], meta[1], meta[2], meta[3], x2d, g1, g2, yb)


def _moe(x2d, norm_g, wg, bg, we, be, layer, w_gate, w_up, w_down):
    t = x2d.shape[0]
    h_tiles, meta, g1, g2, cnt = _router(x2d, norm_g, wg, bg, we, be)
    counts = cnt[0, :N_EXPERTS].astype(jnp.int32)
    padded = ((counts + MOE_BLOCK - 1) // MOE_BLOCK) * MOE_BLOCK
    ends_pad = jnp.cumsum(padded)
    starts_pad = (ends_pad - padded).astype(jnp.int32)
    a = t * TOP_K
    p_rows = (-(-a // MOE_BLOCK)) * MOE_BLOCK + N_EXPERTS * MOE_BLOCK
    n_blocks = p_rows // MOE_BLOCK
    blk_start = jnp.arange(n_blocks, dtype=jnp.int32) * MOE_BLOCK
    blk_e = jnp.minimum(jnp.sum((ends_pad[None, :] <= blk_start[:, None]).astype(jnp.int32), axis=1), N_EXPERTS - 1)
    n_used = (ends_pad[-1:] // MOE_BLOCK).astype(jnp.int32)
    buf = _dispatch(h_tiles, meta, starts_pad, p_rows)
    yb = _experts(buf, blk_e, n_used, layer, w_gate, w_up, w_down)
    return _combine(x2d, g1, g2, yb, meta, starts_pad)


def _split_w_in(w_in):
    sizes = (GROUP_WIDTH, SSD_CONV_CH, SSD_HEADS) + (GROUP_WIDTH,) * 8
    names = ("z", "xbc", "dt", "sc_b", "sc_c", "sc_h", "q", "k", "v", "cf_a", "cf_g")
    parts, o = {}, 0
    for nme, s in zip(names, sizes):
        parts[nme] = w_in[:, o:o + s]
        o += s
    parts["dt"] = jnp.pad(parts["dt"], ((0, 0), (0, LANES - SSD_HEADS)))
    return [parts[nme].astype(bf16) for nme, _ in _SEGS]


def kernel(x, mem, norm_mix, w_in, ssd_conv_w, ssd_conv_b, ssd_dt_bias, ssd_a_log, ssd_d, ssd_norm, sc_conv_w,
           attn_q_norm, attn_k_norm, cf_conv_w, cf_conv_b, cf_ln_g, cf_ln_b, w_out, norm_ca, norm_mem, ca_wq, ca_wkv,
           ca_q_norm, ca_k_norm, ca_wo, norm_ffn, router_group_w, router_group_b, router_expert_w, router_expert_b,
           exp_w_gate, exp_w_up, exp_w_down):
    bsz, seq, d = x.shape
    depth = w_in.shape[0]
    assert d == D_MODEL and seq % ATT_TILE == 0 and seq % TS_SSD == 0 and (bsz * seq) % TM_PROJ == 0
    x2d = x.reshape(bsz * seq, d)
    for l in range(depth):
        proj = _in_projection(x2d, norm_mix[l].reshape(1, -1), _split_w_in(w_in[l]))
        y_ssd = _ssd(proj, bsz, seq, ssd_conv_w[l], ssd_conv_b[l], ssd_dt_bias[l], ssd_a_log[l], ssd_d[l], ssd_norm[l])
        y_sc, y_cf = _convs(proj, bsz, seq, sc_conv_w[l], cf_conv_w[l], cf_conv_b[l], cf_ln_g[l], cf_ln_b[l])
        y_att = _dilated_attention(proj, bsz, seq, attn_q_norm[l], attn_k_norm[l])
        kn, vv = _memory_kv(mem, norm_mem[l], ca_wkv[l].astype(bf16), ca_k_norm[l])
        x2d = _outproj_cross_attention(x2d, (y_ssd, y_sc, y_att, y_cf), w_out[l].astype(bf16), bsz, seq, norm_ca[l],
                                       ca_wq[l].astype(bf16), kn, vv, ca_q_norm[l], ca_wo[l].astype(bf16))
        x2d = _moe(x2d, norm_ffn[l], router_group_w[l], router_group_b[l], router_expert_w[l], router_expert_b[l],
                   l, exp_w_gate, exp_w_up, exp_w_down)
    return x2d.reshape(bsz, seq, d)
```

```python
import functools

import numpy as np
import jax
import jax.numpy as jnp
from jax import lax
from jax.experimental import pallas as pl
from jax.experimental.pallas import tpu as pltpu

f32 = jnp.float32
bf16 = jnp.bfloat16

D_MODEL = 1024
GROUP_WIDTH = 384
SSD_HEADS = 6
SSD_HEAD_DIM = 64
SSD_GROUPS = 2
SSD_STATE = 128
SSD_CONV = 4
SSD_CHUNK = 128
SSD_CONV_CH = GROUP_WIDTH + 2 * SSD_GROUPS * SSD_STATE
SC_WIDTH = 3
ATT_HEADS = 6
ATT_HEAD_DIM = 64
ATT_BLOCK = 128
DILATIONS = (1, 4, 16)
ATT_TILE = ATT_BLOCK * max(DILATIONS)
ATT_GROUPS = {1: 8, 4: 8, 16: 8}
CF_KERNEL = 31
CA_HEADS = 4
CA_HEAD_DIM = D_MODEL // CA_HEADS
N_EXPERT_GROUPS = 4
EXPERTS_PER_GROUP = 8
N_EXPERTS = N_EXPERT_GROUPS * EXPERTS_PER_GROUP
TOP_K = 2
EXPERT_HIDDEN = D_MODEL // 2
MOE_BLOCK = 256
RMS_EPS = 1e-6
LN_EPS = 1e-5

LANES = 128
SUBLANES = 8
VMEM_LIMIT_BYTES = 56 * 1024 * 1024

TM_PROJ = 512
TS_SSD = 1024
SSD_UNROLL = 4
TS_CONV = 512
CONV_ROWS = 64
TM_CA = 512
TM_ROUTER = 512
TD_DISPATCH = 1024
TC_COMBINE = 1024
TOKEN_TILE_ROWS = 8
DMA_UNROLL = 8

_SEGS = (("z", GROUP_WIDTH), ("xbc", SSD_CONV_CH), ("dt", LANES), ("sc_b", GROUP_WIDTH), ("sc_c", GROUP_WIDTH),
         ("sc_h", GROUP_WIDTH), ("q", GROUP_WIDTH), ("k", GROUP_WIDTH), ("v", GROUP_WIDTH), ("cf_a", GROUP_WIDTH),
         ("cf_g", GROUP_WIDTH))
N_PROJ = sum(w for _, w in _SEGS)
N_IN_RAW = N_PROJ - LANES + SSD_HEADS
W_PREP_ROWS = 256


def _params(n_axes):
    return pltpu.CompilerParams(dimension_semantics=("arbitrary",) * n_axes, vmem_limit_bytes=VMEM_LIMIT_BYTES)


def _dot(a, b):
    return jnp.dot(a, b, preferred_element_type=f32)


def _dot_nt(a, b):
    return lax.dot_general(a, b, (((1,), (1,)), ((), ())), preferred_element_type=f32)


def _split2(x):
    hi = x.astype(bf16)
    lo = (x - hi.astype(f32)).astype(bf16)
    return hi, lo


def _split3(x):
    hi = x.astype(bf16)
    r = x - hi.astype(f32)
    mid = r.astype(bf16)
    lo = (r - mid.astype(f32)).astype(bf16)
    return hi, mid, lo


def _dot_exact_rhs(x, m_bf16, n_split):
    parts = _split3(x) if n_split == 3 else _split2(x)
    acc = _dot(parts[0], m_bf16)
    for p in parts[1:]:
        acc = acc + _dot(p, m_bf16)
    return acc


def _sigmoid(x):
    return 1.0 / (1.0 + jnp.exp(-x))


def _silu(x):
    return x * _sigmoid(x)


def _rms(x, gain):
    ms = jnp.mean(x * x, axis=-1, keepdims=True)
    return x * lax.rsqrt(ms + RMS_EPS) * gain


def _inproj_kernel(x_ref, g_ref, w_ref, *o_refs):
    h = _rms(x_ref[...], g_ref[...]).astype(bf16)
    off = 0
    for (_, w), o_ref in zip(_SEGS, o_refs):
        o_ref[...] = _dot(h, w_ref[:, off:off + w])
        off += w


def _in_projection(x2d, gain, layer, w_cat):
    t = x2d.shape[0]
    tm = TM_PROJ
    in_specs = [pl.BlockSpec((tm, D_MODEL), lambda i: (i, 0)), pl.BlockSpec((1, D_MODEL), lambda i: (0, 0)),
                pl.BlockSpec((None, D_MODEL, N_PROJ), lambda i: (layer, 0, 0))]
    out_specs = [pl.BlockSpec((tm, w), lambda i: (i, 0)) for _, w in _SEGS]
    out_shape = [jax.ShapeDtypeStruct((t, w), f32) for _, w in _SEGS]
    outs = pl.pallas_call(
        _inproj_kernel, grid=(t // tm,), in_specs=in_specs, out_specs=out_specs, out_shape=out_shape,
        compiler_params=_params(1), name="in_projection",
    )(x2d, gain, w_cat)
    return dict(zip([n for n, _ in _SEGS], outs))


def _ssd_kernel(xbc_ref, dt_ref, z_ref, cw_ref, cb_ref, dtb_ref, alog_ref, dvec_ref, ng_ref, tril_ref, e384_ref,
                e768_ref, o_ref, xbuf, xc, state):
    ts = TS_SSD
    q = SSD_CHUNK
    halo = SUBLANES

    @pl.when(pl.program_id(1) == 0)
    def _():
        xbuf[0:halo, :] = jnp.zeros((halo, SSD_CONV_CH), f32)
        state[...] = jnp.zeros_like(state)

    xbuf[halo:halo + ts, :] = xbc_ref[...]
    acc = cb_ref[...] + cw_ref[0:1, :] * xbuf[halo - 3:halo - 3 + ts, :]
    for k in range(1, SSD_CONV):
        acc = acc + cw_ref[k:k + 1, :] * xbuf[halo - 3 + k:halo - 3 + k + ts, :]
    xc[...] = _silu(acc)
    xbuf[0:halo, :] = xbuf[ts:ts + halo, :]

    lane_row = lax.broadcasted_iota(jnp.int32, (1, LANES), 1)
    head_lane = lane_row < SSD_HEADS
    a_row = jnp.where(head_lane, -jnp.exp(alog_ref[...]), 0.0)
    lane_sq = lax.broadcasted_iota(jnp.int32, (q, LANES), 1)
    row_sq = lax.broadcasted_iota(jnp.int32, (q, LANES), 0)
    causal = row_sq >= lane_sq
    left = lane_sq < SSD_HEAD_DIM
    tril = tril_ref[...]
    e384 = e384_ref[...]
    e768 = e768_ref[...]
    dvec = dvec_ref[...]
    ng = ng_ref[...]

    def chunk(c, carry):
        r0 = pl.multiple_of(c * q, q)
        dtr = dt_ref[pl.ds(r0, q), :] + dtb_ref[...]
        sp = jnp.maximum(dtr, 0.0) + jnp.log1p(jnp.exp(-jnp.abs(dtr)))
        dt = jnp.where(head_lane, sp, 0.0)
        da = dt * a_row
        d3 = _split3(da)
        cs = _dot(tril, d3[0]) + _dot(tril, d3[1]) + _dot(tril, d3[2])
        cs_t = cs.T
        c3 = _split3(cs)
        cs384 = _dot(c3[0], e384) + _dot(c3[1], e384) + _dot(c3[2], e384)
        cs768 = _dot(c3[0], e768) + _dot(c3[1], e768) + _dot(c3[2], e768)
        dt384 = _dot_exact_rhs(dt, e384, 2)
        last384 = cs384[q - 1:q, :]
        decay384 = jnp.exp(last384 - cs384)
        expcs384 = jnp.exp(cs384)

        xs = xc[pl.ds(r0, q), 0:GROUP_WIDTH]
        bm = xc[pl.ds(r0, q), GROUP_WIDTH:GROUP_WIDTH + SSD_GROUPS * SSD_STATE]
        cm = xc[pl.ds(r0, q), GROUP_WIDTH + SSD_GROUPS * SSD_STATE:SSD_CONV_CH]
        xdt = xs * dt384
        xdt_b = xdt.astype(bf16)
        xdec_b = (xdt * decay384).astype(bf16)
        st = state[...]
        st_b = st.astype(bf16)

        cb_g, bt_g, c_g = [], [], []
        for g in range(SSD_GROUPS):
            b_f = bm[:, g * SSD_STATE:(g + 1) * SSD_STATE]
            c_b = cm[:, g * SSD_STATE:(g + 1) * SSD_STATE].astype(bf16)
            cb_g.append(_dot_nt(c_b, b_f.astype(bf16)))
            bt_g.append(b_f.T.astype(bf16))
            c_g.append(c_b)

        y_parts, st_parts = [], []
        for p in range(SSD_HEADS // 2):
            sl = slice(p * LANES, (p + 1) * LANES)
            xdt_p, xdec_p, st_p = xdt_b[:, sl], xdec_b[:, sl], st_b[:, sl]
            yd, s_new, yo = [], {}, {}
            for hh in range(2):
                h = 2 * p + hh
                g = h // (SSD_HEADS // SSD_GROUPS)
                diff = cs768[:, h * LANES:(h + 1) * LANES] - cs_t[h:h + 1, :]
                seg = jnp.exp(jnp.where(causal, diff, -1e30))
                m = (cb_g[g] * seg).astype(bf16)
                yd.append(_dot(m, xdt_p))
                if g not in s_new:
                    s_new[g] = _dot(bt_g[g], xdec_p)
                    yo[g] = _dot(c_g[g], st_p)
            g0 = (2 * p) // (SSD_HEADS // SSD_GROUPS)
            g1 = (2 * p + 1) // (SSD_HEADS // SSD_GROUPS)
            y_diag = jnp.where(left, yd[0], yd[1])
            y_off = jnp.where(left, yo[g0], yo[g1])
            s_pair = jnp.where(left, s_new[g0], s_new[g1])
            y_parts.append(y_diag + y_off * expcs384[:, sl] + xs[:, sl] * dvec[:, sl])
            st_parts.append(st[:, sl] * expcs384[q - 1:q, sl] + s_pair)
        state[...] = jnp.concatenate(st_parts, axis=-1)
        y = jnp.concatenate(y_parts, axis=-1)
        yg = y * _silu(z_ref[pl.ds(r0, q), :])
        o_ref[pl.ds(r0, q), :] = _rms(yg, ng)
        return carry

    lax.fori_loop(0, ts // q, chunk, 0, unroll=SSD_UNROLL)


def _ssd_constants():
    q = SSD_CHUNK
    tril = np.tril(np.ones((q, q), np.float32))
    e384 = np.zeros((LANES, GROUP_WIDTH), np.float32)
    e768 = np.zeros((LANES, SSD_HEADS * LANES), np.float32)
    for h in range(SSD_HEADS):
        e384[h, h * SSD_HEAD_DIM:(h + 1) * SSD_HEAD_DIM] = 1.0
        e768[h, h * LANES:(h + 1) * LANES] = 1.0
    return jnp.asarray(tril, bf16), jnp.asarray(e384, bf16), jnp.asarray(e768, bf16)


def _ssd(proj, bsz, seq, conv_w, conv_b, dt_bias, a_log, d_skip, norm_g):
    t = bsz * seq
    ts = TS_SSD
    ns = seq // ts
    tril, e384, e768 = _ssd_constants()
    pad = LANES - SSD_HEADS
    dtb = jnp.pad(dt_bias, (0, pad)).reshape(1, LANES)
    alog = jnp.pad(a_log, (0, pad)).reshape(1, LANES)
    dvec = jnp.repeat(d_skip, SSD_HEAD_DIM).reshape(1, GROUP_WIDTH)
    row = lambda b, i: (b * ns + i, 0)
    const = lambda b, i: (0, 0)
    return pl.pallas_call(
        _ssd_kernel, grid=(bsz, ns),
        in_specs=[pl.BlockSpec((ts, SSD_CONV_CH), row), pl.BlockSpec((ts, LANES), row), pl.BlockSpec((ts, GROUP_WIDTH), row),
                  pl.BlockSpec((SSD_CONV, SSD_CONV_CH), const), pl.BlockSpec((1, SSD_CONV_CH), const),
                  pl.BlockSpec((1, LANES), const), pl.BlockSpec((1, LANES), const), pl.BlockSpec((1, GROUP_WIDTH), const),
                  pl.BlockSpec((1, GROUP_WIDTH), const), pl.BlockSpec((SSD_CHUNK, SSD_CHUNK), const),
                  pl.BlockSpec((LANES, GROUP_WIDTH), const), pl.BlockSpec((LANES, SSD_HEADS * LANES), const)],
        out_specs=pl.BlockSpec((ts, GROUP_WIDTH), row),
        out_shape=jax.ShapeDtypeStruct((t, GROUP_WIDTH), f32),
        scratch_shapes=[pltpu.VMEM((ts + SUBLANES, SSD_CONV_CH), f32), pltpu.VMEM((ts, SSD_CONV_CH), f32),
                        pltpu.VMEM((SSD_STATE, GROUP_WIDTH), f32)],
        compiler_params=_params(2), name="ssd_scan",
    )(proj["xbc"], proj["dt"], proj["z"], conv_w, conv_b.reshape(1, -1), dtb, alog, dvec, norm_g.reshape(1, -1), tril,
      e384, e768)


_CF_HALO = 32


def _conv_kernel(scb_ref, scc_ref, sch_ref, cfa_ref, cfg_ref, scw_ref, cfw_ref, cfb_ref, lng_ref, lnb_ref, ysc_ref,
                 ycf_ref, ubuf, gbuf):
    ts = TS_CONV
    uh = SUBLANES

    @pl.when(pl.program_id(1) == 0)
    def _():
        ubuf[0:uh, :] = jnp.zeros((uh, GROUP_WIDTH), f32)
        gbuf[0:_CF_HALO, :] = jnp.zeros((_CF_HALO, GROUP_WIDTH), f32)

    ubuf[uh:uh + ts, :] = scc_ref[...] * sch_ref[...]
    gbuf[_CF_HALO:_CF_HALO + ts, :] = cfa_ref[...] * _sigmoid(cfg_ref[...])

    o = uh - (SC_WIDTH - 1)
    acc = scw_ref[0:1, :] * ubuf[o:o + ts, :]
    for k in range(1, SC_WIDTH):
        acc = acc + scw_ref[k:k + 1, :] * ubuf[o + k:o + k + ts, :]
    ysc_ref[...] = scb_ref[...] * acc

    o = _CF_HALO - (CF_KERNEL - 1)
    cfb = cfb_ref[...]
    lng = lng_ref[...]
    lnb = lnb_ref[...]
    for c in range(ts // CONV_ROWS):
        r0 = c * CONV_ROWS
        acc = cfb
        for s in range(SUBLANES):
            rows = CONV_ROWS + (SUBLANES if s else 0)
            part = None
            for k in range(CF_KERNEL):
                if (o + k) % SUBLANES != s:
                    continue
                a = r0 + o + k - s
                term = cfw_ref[k:k + 1, :] * gbuf[a:a + rows, :]
                part = term if part is None else part + term
            acc = acc + part[s:s + CONV_ROWS, :]
        mu = jnp.mean(acc, axis=-1, keepdims=True)
        xc = acc - mu
        var = jnp.mean(xc * xc, axis=-1, keepdims=True)
        y = xc * lax.rsqrt(var + LN_EPS) * lng + lnb
        ycf_ref[r0:r0 + CONV_ROWS, :] = _silu(y)

    ubuf[0:uh, :] = ubuf[ts:ts + uh, :]
    gbuf[0:_CF_HALO, :] = gbuf[ts:ts + _CF_HALO, :]


def _convs(proj, bsz, seq, sc_w, cf_w, cf_b, ln_g, ln_b):
    t = bsz * seq
    ts = TS_CONV
    ns = seq // ts
    row = lambda b, i: (b * ns + i, 0)
    const = lambda b, i: (0, 0)
    act = pl.BlockSpec((ts, GROUP_WIDTH), row)
    vec = pl.BlockSpec((1, GROUP_WIDTH), const)
    return pl.pallas_call(
        _conv_kernel, grid=(bsz, ns),
        in_specs=[act, act, act, act, act, pl.BlockSpec((SC_WIDTH, GROUP_WIDTH), const),
                  pl.BlockSpec((CF_KERNEL, GROUP_WIDTH), const), vec, vec, vec],
        out_specs=[act, act],
        out_shape=[jax.ShapeDtypeStruct((t, GROUP_WIDTH), f32)] * 2,
        scratch_shapes=[pltpu.VMEM((ts + SUBLANES, GROUP_WIDTH), f32), pltpu.VMEM((ts + _CF_HALO, GROUP_WIDTH), f32)],
        compiler_params=_params(2), name="gated_convs",
    )(proj["sc_b"], proj["sc_c"], proj["sc_h"], proj["cf_a"], proj["cf_g"], sc_w, cf_w, cf_b.reshape(1, -1),
      ln_g.reshape(1, -1), ln_b.reshape(1, -1))


def _attn_kernel(q_ref, k_ref, v_ref, qg_ref, kg_ref, bsum_ref, o_ref, qbuf, kbuf, vbuf, acc_o, acc_m, acc_l):
    ta = ATT_TILE
    n = ATT_BLOCK
    dmax = max(DILATIONS)
    tile = pl.program_id(2)

    @pl.when(tile == 0)
    def _():
        kbuf[0:ta, :] = jnp.zeros((ta, LANES), f32)
        vbuf[0:ta, :] = jnp.zeros((ta, LANES), f32)

    bsum = bsum_ref[...]

    def head_rms(x, g):
        ms = _dot_exact_rhs(x * x, bsum, 2)
        return x * lax.rsqrt(ms + RMS_EPS) * g

    qg = qg_ref[...] * (ATT_HEAD_DIM ** -0.5)
    kg = kg_ref[...]
    for r in range(dmax):
        res = pl.ds(r, n, stride=dmax)
        qbuf[r * n:(r + 1) * n, :] = head_rms(q_ref[res, :], qg)
        kbuf[ta + r * n:ta + (r + 1) * n, :] = head_rms(k_ref[res, :], kg)
        vbuf[ta + r * n:ta + (r + 1) * n, :] = v_ref[res, :]

    qa = lax.broadcasted_iota(jnp.int32, (n, 2 * n), 0)
    kb = lax.broadcasted_iota(jnp.int32, (n, 2 * n), 1)
    cur = kb >= n
    lane = lax.broadcasted_iota(jnp.int32, (n, LANES), 1)
    left = lane < ATT_HEAD_DIM
    ones = jnp.ones((2 * n, LANES), bf16)

    for d in DILATIONS:
        shift = d.bit_length() - 1
        nch = dmax // d
        cr = n // nch
        cshift = cr.bit_length() - 1
        lq = (qa & (cr - 1)) * nch + (qa >> cshift)
        kbl = kb & (n - 1)
        kpos = (kb & n) + (kbl & (cr - 1)) * nch + (kbl >> cshift)
        rel = lq + n - kpos
        band = (rel >= 0) & (rel <= n)
        group = ATT_GROUPS[d]

        def body(it, carry, d=d, shift=shift, nch=nch, cr=cr, band=band, group=group):
            def gather(ref, starts):
                parts = [ref[pl.ds(pl.multiple_of(st, SUBLANES), cr), :] for st in starts]
                return parts[0] if len(parts) == 1 else jnp.concatenate(parts, axis=0)

            loaded = []
            for g in range(group):
                idx = it * group + g
                m = idx >> shift
                r = idx & (d - 1)
                first = m == 0
                pbase = jnp.where(first, 0, ta) + jnp.where(first, nch - 1, m - 1) * cr
                q_st = [(d * c + r) * n + m * cr for c in range(nch)]
                k_st = [pbase + (d * c + r) * n for c in range(nch)] + [ta + st for st in q_st]
                q_r = gather(qbuf, q_st)
                k_r = gather(kbuf, k_st).astype(bf16)
                v_r = jnp.concatenate([gather(vbuf, k_st).astype(bf16), ones], axis=-1)
                old = None
                if d != DILATIONS[0]:
                    old = (gather(acc_o, q_st), gather(acc_m, q_st), gather(acc_l, q_st))
                valid = band & (cur | ((tile > 0) | (m > 0)))
                loaded.append((q_st, q_r, k_r, v_r, old, valid))
            merged = []
            for q_st, q_r, k_r, v_r, old, valid in loaded:
                o_h, m_h, l_h = [], [], []
                for hh in range(2):
                    qm = jnp.where(left if hh == 0 else ~left, q_r, 0.0).astype(bf16)
                    s = jnp.where(valid, _dot_nt(qm, k_r), -1e30)
                    mx = jnp.max(s, axis=-1, keepdims=True)
                    pv = _dot(jnp.exp(s - mx).astype(bf16), v_r)
                    m_h.append(mx)
                    o_h.append(pv[:, :LANES])
                    l_h.append(pv[:, LANES:])
                o_new = jnp.where(left, o_h[0], o_h[1])
                m_new = jnp.where(left, m_h[0], m_h[1])
                l_new = jnp.where(left, l_h[0], l_h[1])
                if old is not None:
                    o_old, m_old, l_old = old
                    m_tot = jnp.maximum(m_old, m_new)
                    a = jnp.exp(m_old - m_tot)
                    b = jnp.exp(m_new - m_tot)
                    o_new = a * o_old + b * o_new
                    l_new = a * l_old + b * l_new
                    m_new = m_tot
                merged.append((q_st, o_new, m_new, l_new))
            for q_st, o_new, m_new, l_new in merged:
                for c, st in enumerate(q_st):
                    dst = pl.ds(pl.multiple_of(st, SUBLANES), cr)
                    acc_o[dst, :] = o_new[c * cr:(c + 1) * cr, :]
                    acc_m[dst, :] = m_new[c * cr:(c + 1) * cr, :]
                    acc_l[dst, :] = l_new[c * cr:(c + 1) * cr, :]
            return carry

        lax.fori_loop(0, ta // n // group, body, 0)

    for r in range(dmax):
        o_ref[pl.ds(r, n, stride=dmax), :] = acc_o[r * n:(r + 1) * n, :] / acc_l[r * n:(r + 1) * n, :]
    kbuf[0:ta, :] = kbuf[ta:2 * ta, :]
    vbuf[0:ta, :] = vbuf[ta:2 * ta, :]


def _dilated_attention(proj, bsz, seq, q_norm, k_norm):
    t = bsz * seq
    ta = ATT_TILE
    nt = seq // ta
    pairs = ATT_HEADS // 2
    bsum = np.zeros((LANES, LANES), np.float32)
    for h in range(2):
        bsum[h * ATT_HEAD_DIM:(h + 1) * ATT_HEAD_DIM, h * ATT_HEAD_DIM:(h + 1) * ATT_HEAD_DIM] = 1.0 / ATT_HEAD_DIM
    blk = pl.BlockSpec((ta, LANES), lambda b, p, i: (b * nt + i, p))
    const = lambda b, p, i: (0, 0)
    vec = pl.BlockSpec((1, LANES), const)
    buf = lambda rows: pltpu.VMEM((rows, LANES), f32)
    return pl.pallas_call(
        _attn_kernel, grid=(bsz, pairs, nt),
        in_specs=[blk, blk, blk, vec, vec, pl.BlockSpec((LANES, LANES), const)],
        out_specs=blk,
        out_shape=jax.ShapeDtypeStruct((t, GROUP_WIDTH), f32),
        scratch_shapes=[buf(ta), buf(2 * ta), buf(2 * ta), buf(ta), buf(ta), buf(ta)],
        compiler_params=_params(3), name="dilated_attention",
    )(proj["q"], proj["k"], proj["v"], jnp.tile(q_norm, 2).reshape(1, LANES), jnp.tile(k_norm, 2).reshape(1, LANES),
      jnp.asarray(bsum, bf16))


def _memkv_kernel(mem_ref, gm_ref, wkv_ref, gk_ref, k_ref, v_ref):
    hm = _rms(mem_ref[...], gm_ref[...]).astype(bf16)
    kv = _dot(hm, wkv_ref[...])
    gk = gk_ref[...]
    for hh in range(CA_HEADS):
        sl = slice(hh * CA_HEAD_DIM, (hh + 1) * CA_HEAD_DIM)
        k_ref[:, sl] = _rms(kv[:, sl], gk).astype(bf16)
    v_ref[...] = kv[:, D_MODEL:].astype(bf16)


def _memory_kv(mem, norm_mem, wkv_b, k_norm):
    bsz, m, _ = mem.shape
    const = lambda b: (0, 0)
    blk = pl.BlockSpec((None, m, D_MODEL), lambda b: (b, 0, 0))
    return pl.pallas_call(
        _memkv_kernel, grid=(bsz,),
        in_specs=[blk, pl.BlockSpec((1, D_MODEL), const), pl.BlockSpec((D_MODEL, 2 * D_MODEL), const),
                  pl.BlockSpec((1, CA_HEAD_DIM), const)],
        out_specs=[blk, blk],
        out_shape=[jax.ShapeDtypeStruct((bsz, m, D_MODEL), bf16)] * 2,
        compiler_params=_params(1), name="memory_kv",
    )(mem, norm_mem.reshape(1, -1), wkv_b, k_norm.reshape(1, -1))


def _outca_kernel(x_ref, y0_ref, y1_ref, y2_ref, y3_ref, w0_ref, w1_ref, w2_ref, w3_ref, gca_ref, wq_ref, kn_ref,
                  v_ref, gq_ref, wo_ref, o_ref):
    x1 = x_ref[...]
    for y_ref, w_ref in ((y0_ref, w0_ref), (y1_ref, w1_ref), (y2_ref, w2_ref), (y3_ref, w3_ref)):
        x1 = x1 + _dot(y_ref[...].astype(bf16), w_ref[...])
    h = _rms(x1, gca_ref[...]).astype(bf16)
    q = _dot(h, wq_ref[...])
    gq = gq_ref[...]
    outs = []
    for hh in range(CA_HEADS):
        sl = slice(hh * CA_HEAD_DIM, (hh + 1) * CA_HEAD_DIM)
        qn = (_rms(q[:, sl], gq) * (CA_HEAD_DIM ** -0.5)).astype(bf16)
        s = _dot_nt(qn, kn_ref[:, sl])
        m = jnp.max(s, axis=-1, keepdims=True)
        p = jnp.exp(s - m)
        l = jnp.sum(p, axis=-1, keepdims=True)
        outs.append((_dot(p.astype(bf16), v_ref[:, sl]) / l).astype(bf16))
    o_ref[...] = x1 + _dot(jnp.concatenate(outs, axis=-1), wo_ref[...])


def _outproj_cross_attention(x2d, ys, w_out_b, bsz, seq, norm_ca, wq_b, kn, vv, q_norm, wo_b):
    t = bsz * seq
    tm = TM_CA
    ns = seq // tm
    m = kn.shape[1]
    row = lambda b, i: (b * ns + i, 0)
    const = lambda b, i: (0, 0)
    mix = pl.BlockSpec((tm, GROUP_WIDTH), row)
    wblk = pl.BlockSpec((GROUP_WIDTH, D_MODEL), const)
    sq = pl.BlockSpec((D_MODEL, D_MODEL), const)
    kvb = pl.BlockSpec((None, m, D_MODEL), lambda b, i: (b, 0, 0))
    w_parts = [w_out_b[j * GROUP_WIDTH:(j + 1) * GROUP_WIDTH] for j in range(4)]
    return pl.pallas_call(
        _outca_kernel, grid=(bsz, ns),
        in_specs=[pl.BlockSpec((tm, D_MODEL), row), mix, mix, mix, mix, wblk, wblk, wblk, wblk,
                  pl.BlockSpec((1, D_MODEL), const), sq, kvb, kvb, pl.BlockSpec((1, CA_HEAD_DIM), const), sq],
        out_specs=pl.BlockSpec((tm, D_MODEL), row),
        out_shape=jax.ShapeDtypeStruct((t, D_MODEL), f32),
        compiler_params=_params(2), name="outproj_cross_attention",
    )(x2d, *ys, *w_parts, norm_ca.reshape(1, -1), wq_b, kn, vv, q_norm.reshape(1, -1), wo_b)


def _to_token_tiles(ref, x):
    rows = x.shape[0]
    for s in range(TOKEN_TILE_ROWS):
        ref[pl.ds(s, rows, stride=TOKEN_TILE_ROWS), :] = x[:, s * LANES:(s + 1) * LANES]


def _from_token_tiles(ref, rows):
    return jnp.concatenate([ref[pl.ds(s, rows, stride=TOKEN_TILE_ROWS), :] for s in range(TOKEN_TILE_ROWS)], axis=-1)


def _router_kernel(x_ref, g_ref, w1_ref, w2_ref, b_ref, slt_ref, h_ref, meta_ref, g1_ref, g2_ref, cnt_ref, base):
    @pl.when(pl.program_id(0) == 0)
    def _():
        base[...] = jnp.zeros_like(base)

    tm = TM_ROUTER
    h = _rms(x_ref[...], g_ref[...])
    _to_token_tiles(h_ref, h)
    h1, h2 = _split2(h)
    w1 = w1_ref[...]
    logits = (_dot(h1, w1) + _dot(h1, w2_ref[...]) + _dot(h2, w1)) + b_ref[...]
    lane = lax.broadcasted_iota(jnp.int32, (tm, LANES), 1)
    lanef = lane.astype(f32)
    neg = -jnp.inf

    lgm = jnp.where((lane >= N_EXPERTS) & (lane < N_EXPERTS + N_EXPERT_GROUPS), logits, neg)
    mg = jnp.max(lgm, axis=-1, keepdims=True)
    g_p = 1.0 / jnp.sum(jnp.exp(lgm - mg), axis=-1, keepdims=True)
    gi = jnp.min(jnp.where(lgm == mg, lanef, float(LANES)), axis=-1, keepdims=True) - float(N_EXPERTS)

    grp = (lane >> (EXPERTS_PER_GROUP.bit_length() - 1)).astype(f32)
    lem = jnp.where((grp == gi) & (lane < N_EXPERTS), logits, neg)
    me = jnp.max(lem, axis=-1, keepdims=True)
    i1 = jnp.min(jnp.where(lem == me, lanef, float(LANES)), axis=-1, keepdims=True)
    le2 = jnp.where(lanef == i1, neg, lem)
    m2 = jnp.max(le2, axis=-1, keepdims=True)
    i2 = jnp.min(jnp.where(le2 == m2, lanef, float(LANES)), axis=-1, keepdims=True)
    e2 = jnp.exp(m2 - me)
    gate1 = g_p / (1.0 + e2)
    gate2 = g_p * e2 / (1.0 + e2)

    oh1 = (lanef == i1).astype(f32)
    oh2 = (lanef == i2).astype(f32)
    cnt = oh1 + oh2
    before = _dot(slt_ref[...], cnt.astype(bf16)) + base[...]
    r1 = jnp.sum(oh1 * before, axis=-1, keepdims=True)
    r2 = jnp.sum(oh2 * before, axis=-1, keepdims=True)
    total = base[...] + jnp.sum(cnt, axis=0, keepdims=True)
    base[...] = total
    cnt_ref[...] = total

    packed = jnp.where(lane == 0, i1, jnp.where(lane == 1, i2, jnp.where(lane == 2, r1, r2)))
    meta_ref[...] = packed.T[0:SUBLANES, :].astype(jnp.int32)
    g1_ref[...] = jnp.broadcast_to(gate1, (tm, LANES))
    g2_ref[...] = jnp.broadcast_to(gate2, (tm, LANES))


def _router(x2d, norm_g, wg, bg, we, be):
    t = x2d.shape[0]
    tm = TM_ROUTER
    pad = LANES - N_EXPERTS - N_EXPERT_GROUPS
    wr = jnp.pad(jnp.concatenate([we, wg], axis=1), ((0, 0), (0, pad)))
    br = jnp.pad(jnp.concatenate([be, bg]), (0, pad)).reshape(1, LANES)
    w1, w2 = _split2(wr)
    slt = jnp.asarray(np.tril(np.ones((tm, tm), np.float32), -1), bf16)
    const = lambda i: (0, 0)
    row = lambda i: (i, 0)
    wspec = pl.BlockSpec((D_MODEL, LANES), const)
    gate = pl.BlockSpec((tm, LANES), row)
    return pl.pallas_call(
        _router_kernel, grid=(t // tm,),
        in_specs=[pl.BlockSpec((tm, D_MODEL), row), pl.BlockSpec((1, D_MODEL), const), wspec, wspec,
                  pl.BlockSpec((1, LANES), const), pl.BlockSpec((tm, tm), const)],
        out_specs=[pl.BlockSpec((tm * TOKEN_TILE_ROWS, LANES), row), pl.BlockSpec((SUBLANES, tm), lambda i: (0, i)),
                   gate, gate, pl.BlockSpec((1, LANES), const)],
        out_shape=[jax.ShapeDtypeStruct((t * TOKEN_TILE_ROWS, LANES), f32), jax.ShapeDtypeStruct((SUBLANES, t), jnp.int32),
                   jax.ShapeDtypeStruct((t, LANES), f32), jax.ShapeDtypeStruct((t, LANES), f32),
                   jax.ShapeDtypeStruct((1, LANES), f32)],
        scratch_shapes=[pltpu.VMEM((1, LANES), f32)],
        compiler_params=_params(1), name="moe_router",
    )(x2d, norm_g.reshape(1, -1), w1, w2, br, slt)


def _tile_copy(src, src_tile, dst, dst_tile, sem):
    rows = TOKEN_TILE_ROWS
    return pltpu.make_async_copy(src.at[pl.ds(pl.multiple_of(src_tile * rows, rows), rows)],
                                 dst.at[pl.ds(pl.multiple_of(dst_tile * rows, rows), rows)], sem)


def _dispatch_kernel(starts_ref, e1_ref, e2_ref, r1_ref, r2_ref, h_ref, zero_ref, buf_ref, sem):
    del zero_ref

    def issue(j, carry):
        _tile_copy(h_ref, j, buf_ref, starts_ref[e1_ref[j]] + r1_ref[j], sem).start()
        _tile_copy(h_ref, j, buf_ref, starts_ref[e2_ref[j]] + r2_ref[j], sem).start()
        return carry

    lax.fori_loop(0, TD_DISPATCH, issue, 0, unroll=DMA_UNROLL)

    def drain(j, carry):
        _tile_copy(h_ref, 0, buf_ref, 0, sem).wait()
        _tile_copy(h_ref, 0, buf_ref, 0, sem).wait()
        return carry

    lax.fori_loop(0, TD_DISPATCH, drain, 0, unroll=DMA_UNROLL)


def _dispatch(h_tiles, meta, starts_pad, p_rows):
    t = meta.shape[1]
    td = TD_DISPATCH
    smem = pl.BlockSpec((td,), lambda i, s: (i,), memory_space=pltpu.SMEM)
    anyspec = pl.BlockSpec(memory_space=pl.ANY)
    hspec = pl.BlockSpec((td * TOKEN_TILE_ROWS, LANES), lambda i, s: (i, 0))
    grid_spec = pltpu.PrefetchScalarGridSpec(
        num_scalar_prefetch=1, grid=(t // td,), in_specs=[smem, smem, smem, smem, hspec, anyspec], out_specs=anyspec,
        scratch_shapes=[pltpu.SemaphoreType.DMA(())])
    shape = (p_rows * TOKEN_TILE_ROWS, LANES)
    return pl.pallas_call(
        _dispatch_kernel, grid_spec=grid_spec, out_shape=jax.ShapeDtypeStruct(shape, f32),
        input_output_aliases={6: 0}, compiler_params=_params(1), name="moe_dispatch",
    )(starts_pad, meta[0], meta[1], meta[2], meta[3], h_tiles, jnp.zeros(shape, f32))


def _expert_kernel(blk_e_ref, nused_ref, x_ref, wg_ref, wu_ref, wd_ref, o_ref):
    del blk_e_ref
    used = pl.program_id(0) < nused_ref[0]

    @pl.when(used)
    def _():
        x = _from_token_tiles(x_ref, MOE_BLOCK).astype(bf16)
        g = _dot(x, wg_ref[...].astype(bf16))
        u = _dot(x, wu_ref[...].astype(bf16))
        _to_token_tiles(o_ref, _dot((_silu(g) * u).astype(bf16), wd_ref[...].astype(bf16)))

    @pl.when(jnp.logical_not(used))
    def _():
        o_ref[...] = jnp.zeros_like(o_ref)


def _experts(buf, blk_e, n_used, layer, w_gate, w_up, w_down):
    n_blocks = buf.shape[0] // (MOE_BLOCK * TOKEN_TILE_ROWS)
    xspec = pl.BlockSpec((MOE_BLOCK * TOKEN_TILE_ROWS, LANES), lambda i, be, nu: (i, 0))
    wspec = lambda a, b: pl.BlockSpec((None, None, a, b), lambda i, be, nu: (layer, be[i], 0, 0))
    grid_spec = pltpu.PrefetchScalarGridSpec(
        num_scalar_prefetch=2, grid=(n_blocks,),
        in_specs=[xspec, wspec(D_MODEL, EXPERT_HIDDEN), wspec(D_MODEL, EXPERT_HIDDEN), wspec(EXPERT_HIDDEN, D_MODEL)],
        out_specs=xspec)
    return pl.pallas_call(
        _expert_kernel, grid_spec=grid_spec, out_shape=jax.ShapeDtypeStruct(buf.shape, f32),
        compiler_params=_params(1), name="moe_experts",
    )(blk_e, n_used, buf, w_gate, w_up, w_down)


def _combine_kernel(starts_ref, e1_ref, e2_ref, r1_ref, r2_ref, x_ref, g1_ref, g2_ref, yb_ref, o_ref, rows0, rows1, sem):
    tc = TC_COMBINE

    def issue(j, carry):
        _tile_copy(yb_ref, starts_ref[e1_ref[j]] + r1_ref[j], rows0, j, sem).start()
        _tile_copy(yb_ref, starts_ref[e2_ref[j]] + r2_ref[j], rows1, j, sem).start()
        return carry

    lax.fori_loop(0, tc, issue, 0, unroll=DMA_UNROLL)

    def drain(j, carry):
        _tile_copy(yb_ref, 0, rows0, 0, sem).wait()
        _tile_copy(yb_ref, 0, rows1, 0, sem).wait()
        return carry

    lax.fori_loop(0, tc, drain, 0, unroll=DMA_UNROLL)
    reps = D_MODEL // LANES
    g1 = jnp.concatenate([g1_ref[...]] * reps, axis=-1)
    g2 = jnp.concatenate([g2_ref[...]] * reps, axis=-1)
    o_ref[...] = x_ref[...] + (g1 * _from_token_tiles(rows0, tc) + g2 * _from_token_tiles(rows1, tc))


def _combine(x2d, g1, g2, yb, meta, starts_pad):
    t = x2d.shape[0]
    tc = TC_COMBINE
    smem = pl.BlockSpec((tc,), lambda i, s: (i,), memory_space=pltpu.SMEM)
    row = lambda i, s: (i, 0)
    tiles = pltpu.VMEM((tc * TOKEN_TILE_ROWS, LANES), f32)
    grid_spec = pltpu.PrefetchScalarGridSpec(
        num_scalar_prefetch=1, grid=(t // tc,),
        in_specs=[smem, smem, smem, smem, pl.BlockSpec((tc, D_MODEL), row), pl.BlockSpec((tc, LANES), row),
                  pl.BlockSpec((tc, LANES), row), pl.BlockSpec(memory_space=pl.ANY)],
        out_specs=pl.BlockSpec((tc, D_MODEL), row),
        scratch_shapes=[tiles, tiles, pltpu.SemaphoreType.DMA(())])
    return pl.pallas_call(
        _combine_kernel, grid_spec=grid_spec, out_shape=jax.ShapeDtypeStruct((t, D_MODEL), f32),
        compiler_params=_params(1), name="moe_combine",
    )(starts_pad, meta[0], meta[1], meta[2], meta[3], x2d, g1, g2, yb)


def _moe(x2d, norm_g, wg, bg, we, be, layer, w_gate, w_up, w_down):
    t = x2d.shape[0]
    h_tiles, meta, g1, g2, cnt = _router(x2d, norm_g, wg, bg, we, be)
    counts = cnt[0, :N_EXPERTS].astype(jnp.int32)
    padded = ((counts + MOE_BLOCK - 1) // MOE_BLOCK) * MOE_BLOCK
    ends_pad = jnp.cumsum(padded)
    starts_pad = (ends_pad - padded).astype(jnp.int32)
    a = t * TOP_K
    p_rows = (-(-a // MOE_BLOCK)) * MOE_BLOCK + N_EXPERTS * MOE_BLOCK
    n_blocks = p_rows // MOE_BLOCK
    blk_start = jnp.arange(n_blocks, dtype=jnp.int32) * MOE_BLOCK
    blk_e = jnp.minimum(jnp.sum((ends_pad[None, :] <= blk_start[:, None]).astype(jnp.int32), axis=1), N_EXPERTS - 1)
    n_used = (ends_pad[-1:] // MOE_BLOCK).astype(jnp.int32)
    buf = _dispatch(h_tiles, meta, starts_pad, p_rows)
    yb = _experts(buf, blk_e, n_used, layer, w_gate, w_up, w_down)
    return _combine(x2d, g1, g2, yb, meta, starts_pad)


def _prep_w_in_kernel(w_ref, o_ref):
    cut = GROUP_WIDTH + SSD_CONV_CH
    o_ref[:, 0:cut] = w_ref[:, 0:cut].astype(bf16)
    lane = lax.broadcasted_iota(jnp.int32, (W_PREP_ROWS, LANES), 1)
    o_ref[:, cut:cut + LANES] = jnp.where(lane < SSD_HEADS, w_ref[:, cut:cut + LANES], 0.0).astype(bf16)
    tail = w_ref[:, cut:N_IN_RAW]
    o_ref[:, cut + LANES:N_PROJ] = tail[:, SSD_HEADS:SSD_HEADS + N_PROJ - cut - LANES].astype(bf16)


def _prep_w_in(w_in):
    depth = w_in.shape[0]
    rows = W_PREP_ROWS
    return pl.pallas_call(
        _prep_w_in_kernel, grid=(depth, D_MODEL // rows),
        in_specs=[pl.BlockSpec((None, rows, N_IN_RAW), lambda l, i: (l, i, 0))],
        out_specs=pl.BlockSpec((None, rows, N_PROJ), lambda l, i: (l, i, 0)),
        out_shape=jax.ShapeDtypeStruct((depth, D_MODEL, N_PROJ), bf16),
        compiler_params=_params(2), name="prep_w_in",
    )(w_in)


def kernel(x, mem, norm_mix, w_in, ssd_conv_w, ssd_conv_b, ssd_dt_bias, ssd_a_log, ssd_d, ssd_norm, sc_conv_w,
           attn_q_norm, attn_k_norm, cf_conv_w, cf_conv_b, cf_ln_g, cf_ln_b, w_out, norm_ca, norm_mem, ca_wq, ca_wkv,
           ca_q_norm, ca_k_norm, ca_wo, norm_ffn, router_group_w, router_group_b, router_expert_w, router_expert_b,
           exp_w_gate, exp_w_up, exp_w_down):
    bsz, seq, d = x.shape
    depth = w_in.shape[0]
    assert d == D_MODEL and seq % ATT_TILE == 0 and seq % TS_SSD == 0 and (bsz * seq) % TM_PROJ == 0
    x2d = x.reshape(bsz * seq, d)
    w_cat = _prep_w_in(w_in)
    for l in range(depth):
        proj = _in_projection(x2d, norm_mix[l].reshape(1, -1), l, w_cat)
        y_ssd = _ssd(proj, bsz, seq, ssd_conv_w[l], ssd_conv_b[l], ssd_dt_bias[l], ssd_a_log[l], ssd_d[l], ssd_norm[l])
        y_sc, y_cf = _convs(proj, bsz, seq, sc_conv_w[l], cf_conv_w[l], cf_conv_b[l], cf_ln_g[l], cf_ln_b[l])
        y_att = _dilated_attention(proj, bsz, seq, attn_q_norm[l], attn_k_norm[l])
        kn, vv = _memory_kv(mem, norm_mem[l], ca_wkv[l].astype(bf16), ca_k_norm[l])
        x2d = _outproj_cross_attention(x2d, (y_ssd, y_sc, y_att, y_cf), w_out[l].astype(bf16), bsz, seq, norm_ca[l],
                                       ca_wq[l].astype(bf16), kn, vv, ca_q_norm[l], ca_wo[l].astype(bf16))
        x2d = _moe(x2d, norm_ffn[l], router_group_w[l], router_group_b[l], router_expert_w[l], router_expert_b[l],
                   l, exp_w_gate, exp_w_up, exp_w_down)
    return x2d.reshape(bsz, seq, d)
```

```python
import functools

import numpy as np
import jax
import jax.numpy as jnp
from jax import lax
from jax.experimental import pallas as pl
from jax.experimental.pallas import tpu as pltpu

f32 = jnp.float32
bf16 = jnp.bfloat16

D_MODEL = 1024
GROUP_WIDTH = 384
SSD_HEADS = 6
SSD_HEAD_DIM = 64
SSD_GROUPS = 2
SSD_STATE = 128
SSD_CONV = 4
SSD_CHUNK = 128
SSD_CONV_CH = GROUP_WIDTH + 2 * SSD_GROUPS * SSD_STATE
SC_WIDTH = 3
ATT_HEADS = 6
ATT_HEAD_DIM = 64
ATT_BLOCK = 128
DILATIONS = (1, 4, 16)
ATT_TILE = ATT_BLOCK * max(DILATIONS)
ATT_GROUPS = {1: 8, 4: 8, 16: 8}
CF_KERNEL = 31
CA_HEADS = 4
CA_HEAD_DIM = D_MODEL // CA_HEADS
N_EXPERT_GROUPS = 4
EXPERTS_PER_GROUP = 8
N_EXPERTS = N_EXPERT_GROUPS * EXPERTS_PER_GROUP
TOP_K = 2
EXPERT_HIDDEN = D_MODEL // 2
MOE_BLOCK = 512
RMS_EPS = 1e-6
LN_EPS = 1e-5

LANES = 128
SUBLANES = 8
VMEM_LIMIT_BYTES = 56 * 1024 * 1024

TM_PROJ = 512
TS_SSD = 1024
SSD_UNROLL = 4
TS_CONV = 512
CONV_ROWS = 64
TM_CA = 512
TM_ROUTER = 512
TD_DISPATCH = 1024
TC_COMBINE = 1024
TOKEN_TILE_ROWS = 8
DMA_UNROLL = 8

_SEGS = (("z", GROUP_WIDTH), ("xbc", SSD_CONV_CH), ("dt", LANES), ("sc_b", GROUP_WIDTH), ("sc_c", GROUP_WIDTH),
         ("sc_h", GROUP_WIDTH), ("q", GROUP_WIDTH), ("k", GROUP_WIDTH), ("v", GROUP_WIDTH), ("cf_a", GROUP_WIDTH),
         ("cf_g", GROUP_WIDTH))
N_PROJ = sum(w for _, w in _SEGS)
N_IN_RAW = N_PROJ - LANES + SSD_HEADS
W_PREP_ROWS = 256


def _params(n_axes):
    return pltpu.CompilerParams(dimension_semantics=("arbitrary",) * n_axes, vmem_limit_bytes=VMEM_LIMIT_BYTES)


def _dot(a, b):
    return jnp.dot(a, b, preferred_element_type=f32)


def _dot_nt(a, b):
    return lax.dot_general(a, b, (((1,), (1,)), ((), ())), preferred_element_type=f32)


def _split2(x):
    hi = x.astype(bf16)
    lo = (x - hi.astype(f32)).astype(bf16)
    return hi, lo


def _split3(x):
    hi = x.astype(bf16)
    r = x - hi.astype(f32)
    mid = r.astype(bf16)
    lo = (r - mid.astype(f32)).astype(bf16)
    return hi, mid, lo


def _dot_exact_rhs(x, m_bf16, n_split):
    parts = _split3(x) if n_split == 3 else _split2(x)
    acc = _dot(parts[0], m_bf16)
    for p in parts[1:]:
        acc = acc + _dot(p, m_bf16)
    return acc


def _sigmoid(x):
    return 1.0 / (1.0 + jnp.exp(-x))


def _silu(x):
    return x * _sigmoid(x)


def _rms(x, gain):
    ms = jnp.mean(x * x, axis=-1, keepdims=True)
    return x * lax.rsqrt(ms + RMS_EPS) * gain


def _inproj_kernel(x_ref, g_ref, w_ref, *o_refs):
    h = _rms(x_ref[...], g_ref[...]).astype(bf16)
    off = 0
    for (_, w), o_ref in zip(_SEGS, o_refs):
        o_ref[...] = _dot(h, w_ref[:, off:off + w])
        off += w


def _in_projection(x2d, gain, layer, w_cat):
    t = x2d.shape[0]
    tm = TM_PROJ
    in_specs = [pl.BlockSpec((tm, D_MODEL), lambda i: (i, 0)), pl.BlockSpec((1, D_MODEL), lambda i: (0, 0)),
                pl.BlockSpec((None, D_MODEL, N_PROJ), lambda i: (layer, 0, 0))]
    out_specs = [pl.BlockSpec((tm, w), lambda i: (i, 0)) for _, w in _SEGS]
    out_shape = [jax.ShapeDtypeStruct((t, w), f32) for _, w in _SEGS]
    outs = pl.pallas_call(
        _inproj_kernel, grid=(t // tm,), in_specs=in_specs, out_specs=out_specs, out_shape=out_shape,
        compiler_params=_params(1), name="in_projection",
    )(x2d, gain, w_cat)
    return dict(zip([n for n, _ in _SEGS], outs))


def _ssd_kernel(xbc_ref, dt_ref, z_ref, cw_ref, cb_ref, dtb_ref, alog_ref, dvec_ref, ng_ref, tril_ref, e384_ref,
                e768_ref, o_ref, xbuf, xc, state):
    ts = TS_SSD
    q = SSD_CHUNK
    halo = SUBLANES

    @pl.when(pl.program_id(1) == 0)
    def _():
        xbuf[0:halo, :] = jnp.zeros((halo, SSD_CONV_CH), f32)
        state[...] = jnp.zeros_like(state)

    xbuf[halo:halo + ts, :] = xbc_ref[...]
    acc = cb_ref[...] + cw_ref[0:1, :] * xbuf[halo - 3:halo - 3 + ts, :]
    for k in range(1, SSD_CONV):
        acc = acc + cw_ref[k:k + 1, :] * xbuf[halo - 3 + k:halo - 3 + k + ts, :]
    xc[...] = _silu(acc)
    xbuf[0:halo, :] = xbuf[ts:ts + halo, :]

    lane_row = lax.broadcasted_iota(jnp.int32, (1, LANES), 1)
    head_lane = lane_row < SSD_HEADS
    a_row = jnp.where(head_lane, -jnp.exp(alog_ref[...]), 0.0)
    lane_sq = lax.broadcasted_iota(jnp.int32, (q, LANES), 1)
    row_sq = lax.broadcasted_iota(jnp.int32, (q, LANES), 0)
    causal = row_sq >= lane_sq
    left = lane_sq < SSD_HEAD_DIM
    tril = tril_ref[...]
    e384 = e384_ref[...]
    e768 = e768_ref[...]
    dvec = dvec_ref[...]
    ng = ng_ref[...]

    def chunk(c, carry):
        r0 = pl.multiple_of(c * q, q)
        dtr = dt_ref[pl.ds(r0, q), :] + dtb_ref[...]
        sp = jnp.maximum(dtr, 0.0) + jnp.log1p(jnp.exp(-jnp.abs(dtr)))
        dt = jnp.where(head_lane, sp, 0.0)
        da = dt * a_row
        d3 = _split3(da)
        cs = _dot(tril, d3[0]) + _dot(tril, d3[1]) + _dot(tril, d3[2])
        cs_t = cs.T
        c3 = _split3(cs)
        cs384 = _dot(c3[0], e384) + _dot(c3[1], e384) + _dot(c3[2], e384)
        cs768 = _dot(c3[0], e768) + _dot(c3[1], e768) + _dot(c3[2], e768)
        dt384 = _dot_exact_rhs(dt, e384, 2)
        last384 = cs384[q - 1:q, :]
        decay384 = jnp.exp(last384 - cs384)
        expcs384 = jnp.exp(cs384)

        xs = xc[pl.ds(r0, q), 0:GROUP_WIDTH]
        bm = xc[pl.ds(r0, q), GROUP_WIDTH:GROUP_WIDTH + SSD_GROUPS * SSD_STATE]
        cm = xc[pl.ds(r0, q), GROUP_WIDTH + SSD_GROUPS * SSD_STATE:SSD_CONV_CH]
        xdt = xs * dt384
        xdt_b = xdt.astype(bf16)
        xdec_b = (xdt * decay384).astype(bf16)
        st = state[...]
        st_b = st.astype(bf16)

        cb_g, bt_g, c_g = [], [], []
        for g in range(SSD_GROUPS):
            b_f = bm[:, g * SSD_STATE:(g + 1) * SSD_STATE]
            c_b = cm[:, g * SSD_STATE:(g + 1) * SSD_STATE].astype(bf16)
            cb_g.append(_dot_nt(c_b, b_f.astype(bf16)))
            bt_g.append(b_f.T.astype(bf16))
            c_g.append(c_b)

        y_parts, st_parts = [], []
        for p in range(SSD_HEADS // 2):
            sl = slice(p * LANES, (p + 1) * LANES)
            xdt_p, xdec_p, st_p = xdt_b[:, sl], xdec_b[:, sl], st_b[:, sl]
            yd, s_new, yo = [], {}, {}
            for hh in range(2):
                h = 2 * p + hh
                g = h // (SSD_HEADS // SSD_GROUPS)
                diff = cs768[:, h * LANES:(h + 1) * LANES] - cs_t[h:h + 1, :]
                seg = jnp.exp(jnp.where(causal, diff, -1e30))
                m = (cb_g[g] * seg).astype(bf16)
                yd.append(_dot(m, xdt_p))
                if g not in s_new:
                    s_new[g] = _dot(bt_g[g], xdec_p)
                    yo[g] = _dot(c_g[g], st_p)
            g0 = (2 * p) // (SSD_HEADS // SSD_GROUPS)
            g1 = (2 * p + 1) // (SSD_HEADS // SSD_GROUPS)
            y_diag = jnp.where(left, yd[0], yd[1])
            y_off = jnp.where(left, yo[g0], yo[g1])
            s_pair = jnp.where(left, s_new[g0], s_new[g1])
            y_parts.append(y_diag + y_off * expcs384[:, sl] + xs[:, sl] * dvec[:, sl])
            st_parts.append(st[:, sl] * expcs384[q - 1:q, sl] + s_pair)
        state[...] = jnp.concatenate(st_parts, axis=-1)
        y = jnp.concatenate(y_parts, axis=-1)
        yg = y * _silu(z_ref[pl.ds(r0, q), :])
        o_ref[pl.ds(r0, q), :] = _rms(yg, ng)
        return carry

    lax.fori_loop(0, ts // q, chunk, 0, unroll=SSD_UNROLL)


def _ssd_constants():
    q = SSD_CHUNK
    tril = np.tril(np.ones((q, q), np.float32))
    e384 = np.zeros((LANES, GROUP_WIDTH), np.float32)
    e768 = np.zeros((LANES, SSD_HEADS * LANES), np.float32)
    for h in range(SSD_HEADS):
        e384[h, h * SSD_HEAD_DIM:(h + 1) * SSD_HEAD_DIM] = 1.0
        e768[h, h * LANES:(h + 1) * LANES] = 1.0
    return jnp.asarray(tril, bf16), jnp.asarray(e384, bf16), jnp.asarray(e768, bf16)


def _ssd(proj, bsz, seq, conv_w, conv_b, dt_bias, a_log, d_skip, norm_g):
    t = bsz * seq
    ts = TS_SSD
    ns = seq // ts
    tril, e384, e768 = _ssd_constants()
    pad = LANES - SSD_HEADS
    dtb = jnp.pad(dt_bias, (0, pad)).reshape(1, LANES)
    alog = jnp.pad(a_log, (0, pad)).reshape(1, LANES)
    dvec = jnp.repeat(d_skip, SSD_HEAD_DIM).reshape(1, GROUP_WIDTH)
    row = lambda b, i: (b * ns + i, 0)
    const = lambda b, i: (0, 0)
    return pl.pallas_call(
        _ssd_kernel, grid=(bsz, ns),
        in_specs=[pl.BlockSpec((ts, SSD_CONV_CH), row), pl.BlockSpec((ts, LANES), row), pl.BlockSpec((ts, GROUP_WIDTH), row),
                  pl.BlockSpec((SSD_CONV, SSD_CONV_CH), const), pl.BlockSpec((1, SSD_CONV_CH), const),
                  pl.BlockSpec((1, LANES), const), pl.BlockSpec((1, LANES), const), pl.BlockSpec((1, GROUP_WIDTH), const),
                  pl.BlockSpec((1, GROUP_WIDTH), const), pl.BlockSpec((SSD_CHUNK, SSD_CHUNK), const),
                  pl.BlockSpec((LANES, GROUP_WIDTH), const), pl.BlockSpec((LANES, SSD_HEADS * LANES), const)],
        out_specs=pl.BlockSpec((ts, GROUP_WIDTH), row),
        out_shape=jax.ShapeDtypeStruct((t, GROUP_WIDTH), f32),
        scratch_shapes=[pltpu.VMEM((ts + SUBLANES, SSD_CONV_CH), f32), pltpu.VMEM((ts, SSD_CONV_CH), f32),
                        pltpu.VMEM((SSD_STATE, GROUP_WIDTH), f32)],
        compiler_params=_params(2), name="ssd_scan",
    )(proj["xbc"], proj["dt"], proj["z"], conv_w, conv_b.reshape(1, -1), dtb, alog, dvec, norm_g.reshape(1, -1), tril,
      e384, e768)


_CF_HALO = 32


def _conv_kernel(scb_ref, scc_ref, sch_ref, cfa_ref, cfg_ref, scw_ref, cfw_ref, cfb_ref, lng_ref, lnb_ref, ysc_ref,
                 ycf_ref, ubuf, gbuf):
    ts = TS_CONV
    uh = SUBLANES

    @pl.when(pl.program_id(1) == 0)
    def _():
        ubuf[0:uh, :] = jnp.zeros((uh, GROUP_WIDTH), f32)
        gbuf[0:_CF_HALO, :] = jnp.zeros((_CF_HALO, GROUP_WIDTH), f32)

    ubuf[uh:uh + ts, :] = scc_ref[...] * sch_ref[...]
    gbuf[_CF_HALO:_CF_HALO + ts, :] = cfa_ref[...] * _sigmoid(cfg_ref[...])

    o = uh - (SC_WIDTH - 1)
    acc = scw_ref[0:1, :] * ubuf[o:o + ts, :]
    for k in range(1, SC_WIDTH):
        acc = acc + scw_ref[k:k + 1, :] * ubuf[o + k:o + k + ts, :]
    ysc_ref[...] = scb_ref[...] * acc

    o = _CF_HALO - (CF_KERNEL - 1)
    cfb = cfb_ref[...]
    lng = lng_ref[...]
    lnb = lnb_ref[...]
    for c in range(ts // CONV_ROWS):
        r0 = c * CONV_ROWS
        acc = cfb
        for s in range(SUBLANES):
            rows = CONV_ROWS + (SUBLANES if s else 0)
            part = None
            for k in range(CF_KERNEL):
                if (o + k) % SUBLANES != s:
                    continue
                a = r0 + o + k - s
                term = cfw_ref[k:k + 1, :] * gbuf[a:a + rows, :]
                part = term if part is None else part + term
            acc = acc + part[s:s + CONV_ROWS, :]
        mu = jnp.mean(acc, axis=-1, keepdims=True)
        xc = acc - mu
        var = jnp.mean(xc * xc, axis=-1, keepdims=True)
        y = xc * lax.rsqrt(var + LN_EPS) * lng + lnb
        ycf_ref[r0:r0 + CONV_ROWS, :] = _silu(y)

    ubuf[0:uh, :] = ubuf[ts:ts + uh, :]
    gbuf[0:_CF_HALO, :] = gbuf[ts:ts + _CF_HALO, :]


def _convs(proj, bsz, seq, sc_w, cf_w, cf_b, ln_g, ln_b):
    t = bsz * seq
    ts = TS_CONV
    ns = seq // ts
    row = lambda b, i: (b * ns + i, 0)
    const = lambda b, i: (0, 0)
    act = pl.BlockSpec((ts, GROUP_WIDTH), row)
    vec = pl.BlockSpec((1, GROUP_WIDTH), const)
    return pl.pallas_call(
        _conv_kernel, grid=(bsz, ns),
        in_specs=[act, act, act, act, act, pl.BlockSpec((SC_WIDTH, GROUP_WIDTH), const),
                  pl.BlockSpec((CF_KERNEL, GROUP_WIDTH), const), vec, vec, vec],
        out_specs=[act, act],
        out_shape=[jax.ShapeDtypeStruct((t, GROUP_WIDTH), f32)] * 2,
        scratch_shapes=[pltpu.VMEM((ts + SUBLANES, GROUP_WIDTH), f32), pltpu.VMEM((ts + _CF_HALO, GROUP_WIDTH), f32)],
        compiler_params=_params(2), name="gated_convs",
    )(proj["sc_b"], proj["sc_c"], proj["sc_h"], proj["cf_a"], proj["cf_g"], sc_w, cf_w, cf_b.reshape(1, -1),
      ln_g.reshape(1, -1), ln_b.reshape(1, -1))


def _attn_kernel(q_ref, k_ref, v_ref, qg_ref, kg_ref, bsum_ref, o_ref, qbuf, kbuf, vbuf, acc_o, acc_m, acc_l):
    ta = ATT_TILE
    n = ATT_BLOCK
    dmax = max(DILATIONS)
    tile = pl.program_id(2)

    @pl.when(tile == 0)
    def _():
        kbuf[0:ta, :] = jnp.zeros((ta, LANES), f32)
        vbuf[0:ta, :] = jnp.zeros((ta, LANES), f32)

    bsum = bsum_ref[...]

    def head_rms(x, g):
        ms = _dot_exact_rhs(x * x, bsum, 2)
        return x * lax.rsqrt(ms + RMS_EPS) * g

    qg = qg_ref[...] * (ATT_HEAD_DIM ** -0.5)
    kg = kg_ref[...]
    for r in range(dmax):
        res = pl.ds(r, n, stride=dmax)
        qbuf[r * n:(r + 1) * n, :] = head_rms(q_ref[res, :], qg)
        kbuf[ta + r * n:ta + (r + 1) * n, :] = head_rms(k_ref[res, :], kg)
        vbuf[ta + r * n:ta + (r + 1) * n, :] = v_ref[res, :]

    qa = lax.broadcasted_iota(jnp.int32, (n, 2 * n), 0)
    kb = lax.broadcasted_iota(jnp.int32, (n, 2 * n), 1)
    cur = kb >= n
    lane = lax.broadcasted_iota(jnp.int32, (n, LANES), 1)
    left = lane < ATT_HEAD_DIM
    ones = jnp.ones((2 * n, LANES), bf16)

    for d in DILATIONS:
        shift = d.bit_length() - 1
        nch = dmax // d
        cr = n // nch
        cshift = cr.bit_length() - 1
        lq = (qa & (cr - 1)) * nch + (qa >> cshift)
        kbl = kb & (n - 1)
        kpos = (kb & n) + (kbl & (cr - 1)) * nch + (kbl >> cshift)
        rel = lq + n - kpos
        band = (rel >= 0) & (rel <= n)
        group = ATT_GROUPS[d]

        def body(it, carry, d=d, shift=shift, nch=nch, cr=cr, band=band, group=group):
            def gather(ref, starts):
                parts = [ref[pl.ds(pl.multiple_of(st, SUBLANES), cr), :] for st in starts]
                return parts[0] if len(parts) == 1 else jnp.concatenate(parts, axis=0)

            loaded = []
            for g in range(group):
                idx = it * group + g
                m = idx >> shift
                r = idx & (d - 1)
                first = m == 0
                pbase = jnp.where(first, 0, ta) + jnp.where(first, nch - 1, m - 1) * cr
                q_st = [(d * c + r) * n + m * cr for c in range(nch)]
                k_st = [pbase + (d * c + r) * n for c in range(nch)] + [ta + st for st in q_st]
                q_r = gather(qbuf, q_st)
                k_r = gather(kbuf, k_st).astype(bf16)
                v_r = jnp.concatenate([gather(vbuf, k_st).astype(bf16), ones], axis=-1)
                old = None
                if d != DILATIONS[0]:
                    old = (gather(acc_o, q_st), gather(acc_m, q_st), gather(acc_l, q_st))
                valid = band & (cur | ((tile > 0) | (m > 0)))
                loaded.append((q_st, q_r, k_r, v_r, old, valid))
            merged = []
            for q_st, q_r, k_r, v_r, old, valid in loaded:
                o_h, m_h, l_h = [], [], []
                for hh in range(2):
                    qm = jnp.where(left if hh == 0 else ~left, q_r, 0.0).astype(bf16)
                    s = jnp.where(valid, _dot_nt(qm, k_r), -1e30)
                    mx = jnp.max(s, axis=-1, keepdims=True)
                    pv = _dot(jnp.exp(s - mx).astype(bf16), v_r)
                    m_h.append(mx)
                    o_h.append(pv[:, :LANES])
                    l_h.append(pv[:, LANES:])
                o_new = jnp.where(left, o_h[0], o_h[1])
                m_new = jnp.where(left, m_h[0], m_h[1])
                l_new = jnp.where(left, l_h[0], l_h[1])
                if old is not None:
                    o_old, m_old, l_old = old
                    m_tot = jnp.maximum(m_old, m_new)
                    a = jnp.exp(m_old - m_tot)
                    b = jnp.exp(m_new - m_tot)
                    o_new = a * o_old + b * o_new
                    l_new = a * l_old + b * l_new
                    m_new = m_tot
                merged.append((q_st, o_new, m_new, l_new))
            for q_st, o_new, m_new, l_new in merged:
                for c, st in enumerate(q_st):
                    dst = pl.ds(pl.multiple_of(st, SUBLANES), cr)
                    acc_o[dst, :] = o_new[c * cr:(c + 1) * cr, :]
                    acc_m[dst, :] = m_new[c * cr:(c + 1) * cr, :]
                    acc_l[dst, :] = l_new[c * cr:(c + 1) * cr, :]
            return carry

        lax.fori_loop(0, ta // n // group, body, 0)

    for r in range(dmax):
        o_ref[pl.ds(r, n, stride=dmax), :] = acc_o[r * n:(r + 1) * n, :] / acc_l[r * n:(r + 1) * n, :]
    kbuf[0:ta, :] = kbuf[ta:2 * ta, :]
    vbuf[0:ta, :] = vbuf[ta:2 * ta, :]


def _dilated_attention(proj, bsz, seq, q_norm, k_norm):
    t = bsz * seq
    ta = ATT_TILE
    nt = seq // ta
    pairs = ATT_HEADS // 2
    bsum = np.zeros((LANES, LANES), np.float32)
    for h in range(2):
        bsum[h * ATT_HEAD_DIM:(h + 1) * ATT_HEAD_DIM, h * ATT_HEAD_DIM:(h + 1) * ATT_HEAD_DIM] = 1.0 / ATT_HEAD_DIM
    blk = pl.BlockSpec((ta, LANES), lambda b, p, i: (b * nt + i, p))
    const = lambda b, p, i: (0, 0)
    vec = pl.BlockSpec((1, LANES), const)
    buf = lambda rows: pltpu.VMEM((rows, LANES), f32)
    return pl.pallas_call(
        _attn_kernel, grid=(bsz, pairs, nt),
        in_specs=[blk, blk, blk, vec, vec, pl.BlockSpec((LANES, LANES), const)],
        out_specs=blk,
        out_shape=jax.ShapeDtypeStruct((t, GROUP_WIDTH), f32),
        scratch_shapes=[buf(ta), buf(2 * ta), buf(2 * ta), buf(ta), buf(ta), buf(ta)],
        compiler_params=_params(3), name="dilated_attention",
    )(proj["q"], proj["k"], proj["v"], jnp.tile(q_norm, 2).reshape(1, LANES), jnp.tile(k_norm, 2).reshape(1, LANES),
      jnp.asarray(bsum, bf16))


def _memkv_kernel(mem_ref, gm_ref, wkv_ref, gk_ref, k_ref, v_ref):
    hm = _rms(mem_ref[...], gm_ref[...]).astype(bf16)
    kv = _dot(hm, wkv_ref[...])
    gk = gk_ref[...]
    for hh in range(CA_HEADS):
        sl = slice(hh * CA_HEAD_DIM, (hh + 1) * CA_HEAD_DIM)
        k_ref[:, sl] = _rms(kv[:, sl], gk).astype(bf16)
    v_ref[...] = kv[:, D_MODEL:].astype(bf16)


def _memory_kv(mem, norm_mem, wkv_b, k_norm):
    bsz, m, _ = mem.shape
    const = lambda b: (0, 0)
    blk = pl.BlockSpec((None, m, D_MODEL), lambda b: (b, 0, 0))
    return pl.pallas_call(
        _memkv_kernel, grid=(bsz,),
        in_specs=[blk, pl.BlockSpec((1, D_MODEL), const), pl.BlockSpec((D_MODEL, 2 * D_MODEL), const),
                  pl.BlockSpec((1, CA_HEAD_DIM), const)],
        out_specs=[blk, blk],
        out_shape=[jax.ShapeDtypeStruct((bsz, m, D_MODEL), bf16)] * 2,
        compiler_params=_params(1), name="memory_kv",
    )(mem, norm_mem.reshape(1, -1), wkv_b, k_norm.reshape(1, -1))


def _outca_kernel(x_ref, y0_ref, y1_ref, y2_ref, y3_ref, w0_ref, w1_ref, w2_ref, w3_ref, gca_ref, wq_ref, kn_ref,
                  v_ref, gq_ref, wo_ref, o_ref):
    x1 = x_ref[...]
    for y_ref, w_ref in ((y0_ref, w0_ref), (y1_ref, w1_ref), (y2_ref, w2_ref), (y3_ref, w3_ref)):
        x1 = x1 + _dot(y_ref[...].astype(bf16), w_ref[...])
    h = _rms(x1, gca_ref[...]).astype(bf16)
    q = _dot(h, wq_ref[...])
    gq = gq_ref[...]
    outs = []
    for hh in range(CA_HEADS):
        sl = slice(hh * CA_HEAD_DIM, (hh + 1) * CA_HEAD_DIM)
        qn = (_rms(q[:, sl], gq) * (CA_HEAD_DIM ** -0.5)).astype(bf16)
        s = _dot_nt(qn, kn_ref[:, sl])
        m = jnp.max(s, axis=-1, keepdims=True)
        p = jnp.exp(s - m)
        l = jnp.sum(p, axis=-1, keepdims=True)
        outs.append((_dot(p.astype(bf16), v_ref[:, sl]) / l).astype(bf16))
    o_ref[...] = x1 + _dot(jnp.concatenate(outs, axis=-1), wo_ref[...])


def _outproj_cross_attention(x2d, ys, w_out_b, bsz, seq, norm_ca, wq_b, kn, vv, q_norm, wo_b):
    t = bsz * seq
    tm = TM_CA
    ns = seq // tm
    m = kn.shape[1]
    row = lambda b, i: (b * ns + i, 0)
    const = lambda b, i: (0, 0)
    mix = pl.BlockSpec((tm, GROUP_WIDTH), row)
    wblk = pl.BlockSpec((GROUP_WIDTH, D_MODEL), const)
    sq = pl.BlockSpec((D_MODEL, D_MODEL), const)
    kvb = pl.BlockSpec((None, m, D_MODEL), lambda b, i: (b, 0, 0))
    w_parts = [w_out_b[j * GROUP_WIDTH:(j + 1) * GROUP_WIDTH] for j in range(4)]
    return pl.pallas_call(
        _outca_kernel, grid=(bsz, ns),
        in_specs=[pl.BlockSpec((tm, D_MODEL), row), mix, mix, mix, mix, wblk, wblk, wblk, wblk,
                  pl.BlockSpec((1, D_MODEL), const), sq, kvb, kvb, pl.BlockSpec((1, CA_HEAD_DIM), const), sq],
        out_specs=pl.BlockSpec((tm, D_MODEL), row),
        out_shape=jax.ShapeDtypeStruct((t, D_MODEL), f32),
        compiler_params=_params(2), name="outproj_cross_attention",
    )(x2d, *ys, *w_parts, norm_ca.reshape(1, -1), wq_b, kn, vv, q_norm.reshape(1, -1), wo_b)


def _to_token_tiles(ref, x):
    rows = x.shape[0]
    for s in range(TOKEN_TILE_ROWS):
        ref[pl.ds(s, rows, stride=TOKEN_TILE_ROWS), :] = x[:, s * LANES:(s + 1) * LANES]


def _from_token_tiles(ref, rows):
    return jnp.concatenate([ref[pl.ds(s, rows, stride=TOKEN_TILE_ROWS), :] for s in range(TOKEN_TILE_ROWS)], axis=-1)


def _router_kernel(x_ref, g_ref, w1_ref, w2_ref, b_ref, slt_ref, h_ref, meta_ref, g1_ref, g2_ref, cnt_ref, base):
    @pl.when(pl.program_id(0) == 0)
    def _():
        base[...] = jnp.zeros_like(base)

    tm = TM_ROUTER
    h = _rms(x_ref[...], g_ref[...])
    _to_token_tiles(h_ref, h)
    h1, h2 = _split2(h)
    w1 = w1_ref[...]
    logits = (_dot(h1, w1) + _dot(h1, w2_ref[...]) + _dot(h2, w1)) + b_ref[...]
    lane = lax.broadcasted_iota(jnp.int32, (tm, LANES), 1)
    lanef = lane.astype(f32)
    neg = -jnp.inf

    lgm = jnp.where((lane >= N_EXPERTS) & (lane < N_EXPERTS + N_EXPERT_GROUPS), logits, neg)
    mg = jnp.max(lgm, axis=-1, keepdims=True)
    g_p = 1.0 / jnp.sum(jnp.exp(lgm - mg), axis=-1, keepdims=True)
    gi = jnp.min(jnp.where(lgm == mg, lanef, float(LANES)), axis=-1, keepdims=True) - float(N_EXPERTS)

    grp = (lane >> (EXPERTS_PER_GROUP.bit_length() - 1)).astype(f32)
    lem = jnp.where((grp == gi) & (lane < N_EXPERTS), logits, neg)
    me = jnp.max(lem, axis=-1, keepdims=True)
    i1 = jnp.min(jnp.where(lem == me, lanef, float(LANES)), axis=-1, keepdims=True)
    le2 = jnp.where(lanef == i1, neg, lem)
    m2 = jnp.max(le2, axis=-1, keepdims=True)
    i2 = jnp.min(jnp.where(le2 == m2, lanef, float(LANES)), axis=-1, keepdims=True)
    e2 = jnp.exp(m2 - me)
    gate1 = g_p / (1.0 + e2)
    gate2 = g_p * e2 / (1.0 + e2)

    oh1 = (lanef == i1).astype(f32)
    oh2 = (lanef == i2).astype(f32)
    cnt = oh1 + oh2
    before = _dot(slt_ref[...], cnt.astype(bf16)) + base[...]
    r1 = jnp.sum(oh1 * before, axis=-1, keepdims=True)
    r2 = jnp.sum(oh2 * before, axis=-1, keepdims=True)
    total = base[...] + jnp.sum(cnt, axis=0, keepdims=True)
    base[...] = total
    cnt_ref[...] = total

    packed = jnp.where(lane == 0, i1, jnp.where(lane == 1, i2, jnp.where(lane == 2, r1, r2)))
    meta_ref[...] = packed.T[0:SUBLANES, :].astype(jnp.int32)
    g1_ref[...] = jnp.broadcast_to(gate1, (tm, LANES))
    g2_ref[...] = jnp.broadcast_to(gate2, (tm, LANES))


def _router(x2d, norm_g, wg, bg, we, be):
    t = x2d.shape[0]
    tm = TM_ROUTER
    pad = LANES - N_EXPERTS - N_EXPERT_GROUPS
    wr = jnp.pad(jnp.concatenate([we, wg], axis=1), ((0, 0), (0, pad)))
    br = jnp.pad(jnp.concatenate([be, bg]), (0, pad)).reshape(1, LANES)
    w1, w2 = _split2(wr)
    slt = jnp.asarray(np.tril(np.ones((tm, tm), np.float32), -1), bf16)
    const = lambda i: (0, 0)
    row = lambda i: (i, 0)
    wspec = pl.BlockSpec((D_MODEL, LANES), const)
    gate = pl.BlockSpec((tm, LANES), row)
    return pl.pallas_call(
        _router_kernel, grid=(t // tm,),
        in_specs=[pl.BlockSpec((tm, D_MODEL), row), pl.BlockSpec((1, D_MODEL), const), wspec, wspec,
                  pl.BlockSpec((1, LANES), const), pl.BlockSpec((tm, tm), const)],
        out_specs=[pl.BlockSpec((tm * TOKEN_TILE_ROWS, LANES), row), pl.BlockSpec((SUBLANES, tm), lambda i: (0, i)),
                   gate, gate, pl.BlockSpec((1, LANES), const)],
        out_shape=[jax.ShapeDtypeStruct((t * TOKEN_TILE_ROWS, LANES), f32), jax.ShapeDtypeStruct((SUBLANES, t), jnp.int32),
                   jax.ShapeDtypeStruct((t, LANES), f32), jax.ShapeDtypeStruct((t, LANES), f32),
                   jax.ShapeDtypeStruct((1, LANES), f32)],
        scratch_shapes=[pltpu.VMEM((1, LANES), f32)],
        compiler_params=_params(1), name="moe_router",
    )(x2d, norm_g.reshape(1, -1), w1, w2, br, slt)


def _tile_copy(src, src_tile, dst, dst_tile, sem):
    rows = TOKEN_TILE_ROWS
    return pltpu.make_async_copy(src.at[pl.ds(pl.multiple_of(src_tile * rows, rows), rows)],
                                 dst.at[pl.ds(pl.multiple_of(dst_tile * rows, rows), rows)], sem)


def _dispatch_kernel(starts_ref, e1_ref, e2_ref, r1_ref, r2_ref, h_ref, zero_ref, buf_ref, sem):
    del zero_ref

    def issue(j, carry):
        _tile_copy(h_ref, j, buf_ref, starts_ref[e1_ref[j]] + r1_ref[j], sem).start(priority=0)
        _tile_copy(h_ref, j, buf_ref, starts_ref[e2_ref[j]] + r2_ref[j], sem).start(priority=1)
        return carry

    lax.fori_loop(0, TD_DISPATCH, issue, 0, unroll=DMA_UNROLL)

    def drain(j, carry):
        _tile_copy(h_ref, 0, buf_ref, 0, sem).wait()
        _tile_copy(h_ref, 0, buf_ref, 0, sem).wait()
        return carry

    lax.fori_loop(0, TD_DISPATCH, drain, 0, unroll=DMA_UNROLL)


def _dispatch(h_tiles, meta, starts_pad, p_rows):
    t = meta.shape[1]
    td = TD_DISPATCH
    smem = pl.BlockSpec((td,), lambda i, s: (i,), memory_space=pltpu.SMEM)
    anyspec = pl.BlockSpec(memory_space=pl.ANY)
    hspec = pl.BlockSpec((td * TOKEN_TILE_ROWS, LANES), lambda i, s: (i, 0))
    grid_spec = pltpu.PrefetchScalarGridSpec(
        num_scalar_prefetch=1, grid=(t // td,), in_specs=[smem, smem, smem, smem, hspec, anyspec], out_specs=anyspec,
        scratch_shapes=[pltpu.SemaphoreType.DMA(())])
    shape = (p_rows * TOKEN_TILE_ROWS, LANES)
    return pl.pallas_call(
        _dispatch_kernel, grid_spec=grid_spec, out_shape=jax.ShapeDtypeStruct(shape, f32),
        input_output_aliases={6: 0}, compiler_params=_params(1), name="moe_dispatch",
    )(starts_pad, meta[0], meta[1], meta[2], meta[3], h_tiles, jnp.zeros(shape, f32))


def _expert_kernel(blk_e_ref, nused_ref, x_ref, wg_ref, wu_ref, wd_ref, o_ref):
    del blk_e_ref
    used = pl.program_id(0) < nused_ref[0]

    @pl.when(used)
    def _():
        x = _from_token_tiles(x_ref, MOE_BLOCK).astype(bf16)
        g = _dot(x, wg_ref[...].astype(bf16))
        u = _dot(x, wu_ref[...].astype(bf16))
        _to_token_tiles(o_ref, _dot((_silu(g) * u).astype(bf16), wd_ref[...].astype(bf16)))

    @pl.when(jnp.logical_not(used))
    def _():
        o_ref[...] = jnp.zeros_like(o_ref)


def _experts(buf, blk_e, n_used, layer, w_gate, w_up, w_down):
    n_blocks = buf.shape[0] // (MOE_BLOCK * TOKEN_TILE_ROWS)
    xspec = pl.BlockSpec((MOE_BLOCK * TOKEN_TILE_ROWS, LANES), lambda i, be, nu: (i, 0))
    wspec = lambda a, b: pl.BlockSpec((None, None, a, b), lambda i, be, nu: (layer, be[i], 0, 0))
    grid_spec = pltpu.PrefetchScalarGridSpec(
        num_scalar_prefetch=2, grid=(n_blocks,),
        in_specs=[xspec, wspec(D_MODEL, EXPERT_HIDDEN), wspec(D_MODEL, EXPERT_HIDDEN), wspec(EXPERT_HIDDEN, D_MODEL)],
        out_specs=xspec)
    return pl.pallas_call(
        _expert_kernel, grid_spec=grid_spec, out_shape=jax.ShapeDtypeStruct(buf.shape, f32),
        compiler_params=_params(1), name="moe_experts",
    )(blk_e, n_used, buf, w_gate, w_up, w_down)


def _combine_kernel(starts_ref, e1_ref, e2_ref, r1_ref, r2_ref, x_ref, g1_ref, g2_ref, yb_ref, o_ref, rows0, rows1, sem):
    tc = TC_COMBINE

    def issue(j, carry):
        _tile_copy(yb_ref, starts_ref[e1_ref[j]] + r1_ref[j], rows0, j, sem).start(priority=0)
        _tile_copy(yb_ref, starts_ref[e2_ref[j]] + r2_ref[j], rows1, j, sem).start(priority=1)
        return carry

    lax.fori_loop(0, tc, issue, 0, unroll=DMA_UNROLL)

    def drain(j, carry):
        _tile_copy(yb_ref, 0, rows0, 0, sem).wait()
        _tile_copy(yb_ref, 0, rows1, 0, sem).wait()
        return carry

    lax.fori_loop(0, tc, drain, 0, unroll=DMA_UNROLL)
    reps = D_MODEL // LANES
    g1 = jnp.concatenate([g1_ref[...]] * reps, axis=-1)
    g2 = jnp.concatenate([g2_ref[...]] * reps, axis=-1)
    o_ref[...] = x_ref[...] + (g1 * _from_token_tiles(rows0, tc) + g2 * _from_token_tiles(rows1, tc))


def _combine(x2d, g1, g2, yb, meta, starts_pad):
    t = x2d.shape[0]
    tc = TC_COMBINE
    smem = pl.BlockSpec((tc,), lambda i, s: (i,), memory_space=pltpu.SMEM)
    row = lambda i, s: (i, 0)
    tiles = pltpu.VMEM((tc * TOKEN_TILE_ROWS, LANES), f32)
    grid_spec = pltpu.PrefetchScalarGridSpec(
        num_scalar_prefetch=1, grid=(t // tc,),
        in_specs=[smem, smem, smem, smem, pl.BlockSpec((tc, D_MODEL), row), pl.BlockSpec((tc, LANES), row),
                  pl.BlockSpec((tc, LANES), row), pl.BlockSpec(memory_space=pl.ANY)],
        out_specs=pl.BlockSpec((tc, D_MODEL), row),
        scratch_shapes=[tiles, tiles, pltpu.SemaphoreType.DMA(())])
    return pl.pallas_call(
        _combine_kernel, grid_spec=grid_spec, out_shape=jax.ShapeDtypeStruct((t, D_MODEL), f32),
        compiler_params=_params(1), name="moe_combine",
    )(starts_pad, meta[0], meta[1], meta[2], meta[3], x2d, g1, g2, yb)


def _moe(x2d, norm_g, wg, bg, we, be, layer, w_gate, w_up, w_down):
    t = x2d.shape[0]
    h_tiles, meta, g1, g2, cnt = _router(x2d, norm_g, wg, bg, we, be)
    counts = cnt[0, :N_EXPERTS].astype(jnp.int32)
    padded = ((counts + MOE_BLOCK - 1) // MOE_BLOCK) * MOE_BLOCK
    ends_pad = jnp.cumsum(padded)
    starts_pad = (ends_pad - padded).astype(jnp.int32)
    a = t * TOP_K
    p_rows = (-(-a // MOE_BLOCK)) * MOE_BLOCK + N_EXPERTS * MOE_BLOCK
    n_blocks = p_rows // MOE_BLOCK
    blk_start = jnp.arange(n_blocks, dtype=jnp.int32) * MOE_BLOCK
    blk_e = jnp.minimum(jnp.sum((ends_pad[None, :] <= blk_start[:, None]).astype(jnp.int32), axis=1), N_EXPERTS - 1)
    n_used = (ends_pad[-1:] // MOE_BLOCK).astype(jnp.int32)
    buf = _dispatch(h_tiles, meta, starts_pad, p_rows)
    yb = _experts(buf, blk_e, n_used, layer, w_gate, w_up, w_down)
    return _combine(x2d, g1, g2, yb, meta, starts_pad)


def _prep_w_in_kernel(w_ref, o_ref):
    cut = GROUP_WIDTH + SSD_CONV_CH
    o_ref[:, 0:cut] = w_ref[:, 0:cut].astype(bf16)
    lane = lax.broadcasted_iota(jnp.int32, (W_PREP_ROWS, LANES), 1)
    o_ref[:, cut:cut + LANES] = jnp.where(lane < SSD_HEADS, w_ref[:, cut:cut + LANES], 0.0).astype(bf16)
    tail = w_ref[:, cut:N_IN_RAW]
    o_ref[:, cut + LANES:N_PROJ] = tail[:, SSD_HEADS:SSD_HEADS + N_PROJ - cut - LANES].astype(bf16)


def _prep_w_in(w_in):
    depth = w_in.shape[0]
    rows = W_PREP_ROWS
    return pl.pallas_call(
        _prep_w_in_kernel, grid=(depth, D_MODEL // rows),
        in_specs=[pl.BlockSpec((None, rows, N_IN_RAW), lambda l, i: (l, i, 0))],
        out_specs=pl.BlockSpec((None, rows, N_PROJ), lambda l, i: (l, i, 0)),
        out_shape=jax.ShapeDtypeStruct((depth, D_MODEL, N_PROJ), bf16),
        compiler_params=_params(2), name="prep_w_in",
    )(w_in)


def kernel(x, mem, norm_mix, w_in, ssd_conv_w, ssd_conv_b, ssd_dt_bias, ssd_a_log, ssd_d, ssd_norm, sc_conv_w,
           attn_q_norm, attn_k_norm, cf_conv_w, cf_conv_b, cf_ln_g, cf_ln_b, w_out, norm_ca, norm_mem, ca_wq, ca_wkv,
           ca_q_norm, ca_k_norm, ca_wo, norm_ffn, router_group_w, router_group_b, router_expert_w, router_expert_b,
           exp_w_gate, exp_w_up, exp_w_down):
    bsz, seq, d = x.shape
    depth = w_in.shape[0]
    assert d == D_MODEL and seq % ATT_TILE == 0 and seq % TS_SSD == 0 and (bsz * seq) % TM_PROJ == 0
    x2d = x.reshape(bsz * seq, d)
    w_cat = _prep_w_in(w_in)
    for l in range(depth):
        proj = _in_projection(x2d, norm_mix[l].reshape(1, -1), l, w_cat)
        y_ssd = _ssd(proj, bsz, seq, ssd_conv_w[l], ssd_conv_b[l], ssd_dt_bias[l], ssd_a_log[l], ssd_d[l], ssd_norm[l])
        y_sc, y_cf = _convs(proj, bsz, seq, sc_conv_w[l], cf_conv_w[l], cf_conv_b[l], cf_ln_g[l], cf_ln_b[l])
        y_att = _dilated_attention(proj, bsz, seq, attn_q_norm[l], attn_k_norm[l])
        kn, vv = _memory_kv(mem, norm_mem[l], ca_wkv[l].astype(bf16), ca_k_norm[l])
        x2d = _outproj_cross_attention(x2d, (y_ssd, y_sc, y_att, y_cf), w_out[l].astype(bf16), bsz, seq, norm_ca[l],
                                       ca_wq[l].astype(bf16), kn, vv, ca_q_norm[l], ca_wo[l].astype(bf16))
        x2d = _moe(x2d, norm_ffn[l], router_group_w[l], router_group_b[l], router_expert_w[l], router_expert_b[l],
                   l, exp_w_gate, exp_w_up, exp_w_down)
    return x2d.reshape(bsz, seq, d)
```

```python
import functools

import numpy as np
import jax
import jax.numpy as jnp
from jax import lax
from jax.experimental import pallas as pl
from jax.experimental.pallas import tpu as pltpu

f32 = jnp.float32
bf16 = jnp.bfloat16

D_MODEL = 1024
GROUP_WIDTH = 384
SSD_HEADS = 6
SSD_HEAD_DIM = 64
SSD_GROUPS = 2
SSD_STATE = 128
SSD_CONV = 4
SSD_CHUNK = 128
SSD_CONV_CH = GROUP_WIDTH + 2 * SSD_GROUPS * SSD_STATE
SC_WIDTH = 3
ATT_HEADS = 6
ATT_HEAD_DIM = 64
ATT_BLOCK = 128
DILATIONS = (1, 4, 16)
ATT_TILE = ATT_BLOCK * max(DILATIONS)
ATT_GROUPS = {1: 8, 4: 8, 16: 8}
CF_KERNEL = 31
CA_HEADS = 4
CA_HEAD_DIM = D_MODEL // CA_HEADS
N_EXPERT_GROUPS = 4
EXPERTS_PER_GROUP = 8
N_EXPERTS = N_EXPERT_GROUPS * EXPERTS_PER_GROUP
TOP_K = 2
EXPERT_HIDDEN = D_MODEL // 2
MOE_BLOCK = 512
RMS_EPS = 1e-6
LN_EPS = 1e-5

LANES = 128
SUBLANES = 8
VMEM_LIMIT_BYTES = 56 * 1024 * 1024

TM_PROJ = 512
TS_SSD = 1024
SSD_UNROLL = 8
TS_CONV = 512
CONV_ROWS = 64
TM_CA = 512
TM_ROUTER = 512
TD_DISPATCH = 1024
TC_COMBINE = 1024
TOKEN_TILE_ROWS = 8
DMA_UNROLL = 8

_SEGS = (("z", GROUP_WIDTH), ("xbc", SSD_CONV_CH), ("dt", LANES), ("sc_b", GROUP_WIDTH), ("sc_c", GROUP_WIDTH),
         ("sc_h", GROUP_WIDTH), ("q", GROUP_WIDTH), ("k", GROUP_WIDTH), ("v", GROUP_WIDTH), ("cf_a", GROUP_WIDTH),
         ("cf_g", GROUP_WIDTH))
N_PROJ = sum(w for _, w in _SEGS)
N_IN_RAW = N_PROJ - LANES + SSD_HEADS
W_PREP_ROWS = 256


def _params(n_axes):
    return pltpu.CompilerParams(dimension_semantics=("arbitrary",) * n_axes, vmem_limit_bytes=VMEM_LIMIT_BYTES)


def _dot(a, b):
    return jnp.dot(a, b, preferred_element_type=f32)


def _dot_nt(a, b):
    return lax.dot_general(a, b, (((1,), (1,)), ((), ())), preferred_element_type=f32)


def _split2(x):
    hi = x.astype(bf16)
    lo = (x - hi.astype(f32)).astype(bf16)
    return hi, lo


def _split3(x):
    hi = x.astype(bf16)
    r = x - hi.astype(f32)
    mid = r.astype(bf16)
    lo = (r - mid.astype(f32)).astype(bf16)
    return hi, mid, lo


def _dot_exact_rhs(x, m_bf16, n_split):
    parts = _split3(x) if n_split == 3 else _split2(x)
    acc = _dot(parts[0], m_bf16)
    for p in parts[1:]:
        acc = acc + _dot(p, m_bf16)
    return acc


def _sigmoid(x):
    return 1.0 / (1.0 + jnp.exp(-x))


def _silu(x):
    return x * _sigmoid(x)


def _rms(x, gain):
    ms = jnp.mean(x * x, axis=-1, keepdims=True)
    return x * lax.rsqrt(ms + RMS_EPS) * gain


def _inproj_kernel(x_ref, g_ref, w_ref, *o_refs):
    h = _rms(x_ref[...], g_ref[...]).astype(bf16)
    off = 0
    for (_, w), o_ref in zip(_SEGS, o_refs):
        o_ref[...] = _dot(h, w_ref[:, off:off + w])
        off += w


def _in_projection(x2d, gain, layer, w_cat):
    t = x2d.shape[0]
    tm = TM_PROJ
    in_specs = [pl.BlockSpec((tm, D_MODEL), lambda i: (i, 0)), pl.BlockSpec((1, D_MODEL), lambda i: (0, 0)),
                pl.BlockSpec((None, D_MODEL, N_PROJ), lambda i: (layer, 0, 0))]
    out_specs = [pl.BlockSpec((tm, w), lambda i: (i, 0)) for _, w in _SEGS]
    out_shape = [jax.ShapeDtypeStruct((t, w), f32) for _, w in _SEGS]
    outs = pl.pallas_call(
        _inproj_kernel, grid=(t // tm,), in_specs=in_specs, out_specs=out_specs, out_shape=out_shape,
        compiler_params=_params(1), name="in_projection",
    )(x2d, gain, w_cat)
    return dict(zip([n for n, _ in _SEGS], outs))


def _ssd_kernel(xbc_ref, dt_ref, z_ref, cw_ref, cb_ref, dtb_ref, alog_ref, dvec_ref, ng_ref, tril_ref, e384_ref,
                e768_ref, o_ref, xbuf, xc, state):
    ts = TS_SSD
    q = SSD_CHUNK
    halo = SUBLANES

    @pl.when(pl.program_id(1) == 0)
    def _():
        xbuf[0:halo, :] = jnp.zeros((halo, SSD_CONV_CH), f32)
        state[...] = jnp.zeros_like(state)

    xbuf[halo:halo + ts, :] = xbc_ref[...]
    acc = cb_ref[...] + cw_ref[0:1, :] * xbuf[halo - 3:halo - 3 + ts, :]
    for k in range(1, SSD_CONV):
        acc = acc + cw_ref[k:k + 1, :] * xbuf[halo - 3 + k:halo - 3 + k + ts, :]
    xc[...] = _silu(acc)
    xbuf[0:halo, :] = xbuf[ts:ts + halo, :]

    lane_row = lax.broadcasted_iota(jnp.int32, (1, LANES), 1)
    head_lane = lane_row < SSD_HEADS
    a_row = jnp.where(head_lane, -jnp.exp(alog_ref[...]), 0.0)
    lane_sq = lax.broadcasted_iota(jnp.int32, (q, LANES), 1)
    row_sq = lax.broadcasted_iota(jnp.int32, (q, LANES), 0)
    causal = row_sq >= lane_sq
    left = lane_sq < SSD_HEAD_DIM
    tril = tril_ref[...]
    e384 = e384_ref[...]
    e768 = e768_ref[...]
    dvec = dvec_ref[...]
    ng = ng_ref[...]

    def chunk(c, carry):
        r0 = pl.multiple_of(c * q, q)
        dtr = dt_ref[pl.ds(r0, q), :] + dtb_ref[...]
        sp = jnp.maximum(dtr, 0.0) + jnp.log1p(jnp.exp(-jnp.abs(dtr)))
        dt = jnp.where(head_lane, sp, 0.0)
        da = dt * a_row
        d3 = _split3(da)
        cs = _dot(tril, d3[0]) + _dot(tril, d3[1]) + _dot(tril, d3[2])
        cs_t = cs.T
        c3 = _split3(cs)
        cs384 = _dot(c3[0], e384) + _dot(c3[1], e384) + _dot(c3[2], e384)
        cs768 = _dot(c3[0], e768) + _dot(c3[1], e768) + _dot(c3[2], e768)
        dt384 = _dot_exact_rhs(dt, e384, 2)
        last384 = cs384[q - 1:q, :]
        decay384 = jnp.exp(last384 - cs384)
        expcs384 = jnp.exp(cs384)

        xs = xc[pl.ds(r0, q), 0:GROUP_WIDTH]
        bm = xc[pl.ds(r0, q), GROUP_WIDTH:GROUP_WIDTH + SSD_GROUPS * SSD_STATE]
        cm = xc[pl.ds(r0, q), GROUP_WIDTH + SSD_GROUPS * SSD_STATE:SSD_CONV_CH]
        xdt = xs * dt384
        xdt_b = xdt.astype(bf16)
        xdec_b = (xdt * decay384).astype(bf16)
        st = state[...]
        st_b = st.astype(bf16)

        cb_g, bt_g, c_g = [], [], []
        for g in range(SSD_GROUPS):
            b_f = bm[:, g * SSD_STATE:(g + 1) * SSD_STATE]
            c_b = cm[:, g * SSD_STATE:(g + 1) * SSD_STATE].astype(bf16)
            cb_g.append(_dot_nt(c_b, b_f.astype(bf16)))
            bt_g.append(b_f.T.astype(bf16))
            c_g.append(c_b)

        y_parts, st_parts = [], []
        for p in range(SSD_HEADS // 2):
            sl = slice(p * LANES, (p + 1) * LANES)
            xdt_p, xdec_p, st_p = xdt_b[:, sl], xdec_b[:, sl], st_b[:, sl]
            yd, s_new, yo = [], {}, {}
            for hh in range(2):
                h = 2 * p + hh
                g = h // (SSD_HEADS // SSD_GROUPS)
                diff = cs768[:, h * LANES:(h + 1) * LANES] - cs_t[h:h + 1, :]
                seg = jnp.exp(jnp.where(causal, diff, -1e30))
                m = (cb_g[g] * seg).astype(bf16)
                yd.append(_dot(m, xdt_p))
                if g not in s_new:
                    s_new[g] = _dot(bt_g[g], xdec_p)
                    yo[g] = _dot(c_g[g], st_p)
            g0 = (2 * p) // (SSD_HEADS // SSD_GROUPS)
            g1 = (2 * p + 1) // (SSD_HEADS // SSD_GROUPS)
            y_diag = jnp.where(left, yd[0], yd[1])
            y_off = jnp.where(left, yo[g0], yo[g1])
            s_pair = jnp.where(left, s_new[g0], s_new[g1])
            y_parts.append(y_diag + y_off * expcs384[:, sl] + xs[:, sl] * dvec[:, sl])
            st_parts.append(st[:, sl] * expcs384[q - 1:q, sl] + s_pair)
        state[...] = jnp.concatenate(st_parts, axis=-1)
        y = jnp.concatenate(y_parts, axis=-1)
        yg = y * _silu(z_ref[pl.ds(r0, q), :])
        o_ref[pl.ds(r0, q), :] = _rms(yg, ng)
        return carry

    lax.fori_loop(0, ts // q, chunk, 0, unroll=SSD_UNROLL)


def _ssd_constants():
    q = SSD_CHUNK
    tril = np.tril(np.ones((q, q), np.float32))
    e384 = np.zeros((LANES, GROUP_WIDTH), np.float32)
    e768 = np.zeros((LANES, SSD_HEADS * LANES), np.float32)
    for h in range(SSD_HEADS):
        e384[h, h * SSD_HEAD_DIM:(h + 1) * SSD_HEAD_DIM] = 1.0
        e768[h, h * LANES:(h + 1) * LANES] = 1.0
    return jnp.asarray(tril, bf16), jnp.asarray(e384, bf16), jnp.asarray(e768, bf16)


def _ssd(proj, bsz, seq, conv_w, conv_b, dt_bias, a_log, d_skip, norm_g):
    t = bsz * seq
    ts = TS_SSD
    ns = seq // ts
    tril, e384, e768 = _ssd_constants()
    pad = LANES - SSD_HEADS
    dtb = jnp.pad(dt_bias, (0, pad)).reshape(1, LANES)
    alog = jnp.pad(a_log, (0, pad)).reshape(1, LANES)
    dvec = jnp.repeat(d_skip, SSD_HEAD_DIM).reshape(1, GROUP_WIDTH)
    row = lambda b, i: (b * ns + i, 0)
    const = lambda b, i: (0, 0)
    return pl.pallas_call(
        _ssd_kernel, grid=(bsz, ns),
        in_specs=[pl.BlockSpec((ts, SSD_CONV_CH), row), pl.BlockSpec((ts, LANES), row), pl.BlockSpec((ts, GROUP_WIDTH), row),
                  pl.BlockSpec((SSD_CONV, SSD_CONV_CH), const), pl.BlockSpec((1, SSD_CONV_CH), const),
                  pl.BlockSpec((1, LANES), const), pl.BlockSpec((1, LANES), const), pl.BlockSpec((1, GROUP_WIDTH), const),
                  pl.BlockSpec((1, GROUP_WIDTH), const), pl.BlockSpec((SSD_CHUNK, SSD_CHUNK), const),
                  pl.BlockSpec((LANES, GROUP_WIDTH), const), pl.BlockSpec((LANES, SSD_HEADS * LANES), const)],
        out_specs=pl.BlockSpec((ts, GROUP_WIDTH), row),
        out_shape=jax.ShapeDtypeStruct((t, GROUP_WIDTH), f32),
        scratch_shapes=[pltpu.VMEM((ts + SUBLANES, SSD_CONV_CH), f32), pltpu.VMEM((ts, SSD_CONV_CH), f32),
                        pltpu.VMEM((SSD_STATE, GROUP_WIDTH), f32)],
        compiler_params=_params(2), name="ssd_scan",
    )(proj["xbc"], proj["dt"], proj["z"], conv_w, conv_b.reshape(1, -1), dtb, alog, dvec, norm_g.reshape(1, -1), tril,
      e384, e768)


_CF_HALO = 32


def _conv_kernel(scb_ref, scc_ref, sch_ref, cfa_ref, cfg_ref, scw_ref, cfw_ref, cfb_ref, lng_ref, lnb_ref, ysc_ref,
                 ycf_ref, ubuf, gbuf):
    ts = TS_CONV
    uh = SUBLANES

    @pl.when(pl.program_id(1) == 0)
    def _():
        ubuf[0:uh, :] = jnp.zeros((uh, GROUP_WIDTH), f32)
        gbuf[0:_CF_HALO, :] = jnp.zeros((_CF_HALO, GROUP_WIDTH), f32)

    ubuf[uh:uh + ts, :] = scc_ref[...] * sch_ref[...]
    gbuf[_CF_HALO:_CF_HALO + ts, :] = cfa_ref[...] * _sigmoid(cfg_ref[...])

    o = uh - (SC_WIDTH - 1)
    acc = scw_ref[0:1, :] * ubuf[o:o + ts, :]
    for k in range(1, SC_WIDTH):
        acc = acc + scw_ref[k:k + 1, :] * ubuf[o + k:o + k + ts, :]
    ysc_ref[...] = scb_ref[...] * acc

    o = _CF_HALO - (CF_KERNEL - 1)
    cfb = cfb_ref[...]
    lng = lng_ref[...]
    lnb = lnb_ref[...]
    for c in range(ts // CONV_ROWS):
        r0 = c * CONV_ROWS
        acc = cfb
        for s in range(SUBLANES):
            rows = CONV_ROWS + (SUBLANES if s else 0)
            part = None
            for k in range(CF_KERNEL):
                if (o + k) % SUBLANES != s:
                    continue
                a = r0 + o + k - s
                term = cfw_ref[k:k + 1, :] * gbuf[a:a + rows, :]
                part = term if part is None else part + term
            acc = acc + part[s:s + CONV_ROWS, :]
        mu = jnp.mean(acc, axis=-1, keepdims=True)
        xc = acc - mu
        var = jnp.mean(xc * xc, axis=-1, keepdims=True)
        y = xc * lax.rsqrt(var + LN_EPS) * lng + lnb
        ycf_ref[r0:r0 + CONV_ROWS, :] = _silu(y)

    ubuf[0:uh, :] = ubuf[ts:ts + uh, :]
    gbuf[0:_CF_HALO, :] = gbuf[ts:ts + _CF_HALO, :]


def _convs(proj, bsz, seq, sc_w, cf_w, cf_b, ln_g, ln_b):
    t = bsz * seq
    ts = TS_CONV
    ns = seq // ts
    row = lambda b, i: (b * ns + i, 0)
    const = lambda b, i: (0, 0)
    act = pl.BlockSpec((ts, GROUP_WIDTH), row)
    vec = pl.BlockSpec((1, GROUP_WIDTH), const)
    return pl.pallas_call(
        _conv_kernel, grid=(bsz, ns),
        in_specs=[act, act, act, act, act, pl.BlockSpec((SC_WIDTH, GROUP_WIDTH), const),
                  pl.BlockSpec((CF_KERNEL, GROUP_WIDTH), const), vec, vec, vec],
        out_specs=[act, act],
        out_shape=[jax.ShapeDtypeStruct((t, GROUP_WIDTH), f32)] * 2,
        scratch_shapes=[pltpu.VMEM((ts + SUBLANES, GROUP_WIDTH), f32), pltpu.VMEM((ts + _CF_HALO, GROUP_WIDTH), f32)],
        compiler_params=_params(2), name="gated_convs",
    )(proj["sc_b"], proj["sc_c"], proj["sc_h"], proj["cf_a"], proj["cf_g"], sc_w, cf_w, cf_b.reshape(1, -1),
      ln_g.reshape(1, -1), ln_b.reshape(1, -1))


def _attn_kernel(q_ref, k_ref, v_ref, qg_ref, kg_ref, bsum_ref, o_ref, qbuf, kbuf, vbuf, acc_o, acc_m, acc_l):
    ta = ATT_TILE
    n = ATT_BLOCK
    dmax = max(DILATIONS)
    tile = pl.program_id(2)

    @pl.when(tile == 0)
    def _():
        kbuf[0:ta, :] = jnp.zeros((ta, LANES), f32)
        vbuf[0:ta, :] = jnp.zeros((ta, LANES), f32)

    bsum = bsum_ref[...]

    def head_rms(x, g):
        ms = _dot_exact_rhs(x * x, bsum, 2)
        return x * lax.rsqrt(ms + RMS_EPS) * g

    qg = qg_ref[...] * (ATT_HEAD_DIM ** -0.5)
    kg = kg_ref[...]
    for r in range(dmax):
        res = pl.ds(r, n, stride=dmax)
        qbuf[r * n:(r + 1) * n, :] = head_rms(q_ref[res, :], qg)
        kbuf[ta + r * n:ta + (r + 1) * n, :] = head_rms(k_ref[res, :], kg)
        vbuf[ta + r * n:ta + (r + 1) * n, :] = v_ref[res, :]

    qa = lax.broadcasted_iota(jnp.int32, (n, 2 * n), 0)
    kb = lax.broadcasted_iota(jnp.int32, (n, 2 * n), 1)
    cur = kb >= n
    lane = lax.broadcasted_iota(jnp.int32, (n, LANES), 1)
    left = lane < ATT_HEAD_DIM
    ones = jnp.ones((2 * n, LANES), bf16)

    for d in DILATIONS:
        shift = d.bit_length() - 1
        nch = dmax // d
        cr = n // nch
        cshift = cr.bit_length() - 1
        lq = (qa & (cr - 1)) * nch + (qa >> cshift)
        kbl = kb & (n - 1)
        kpos = (kb & n) + (kbl & (cr - 1)) * nch + (kbl >> cshift)
        rel = lq + n - kpos
        band = (rel >= 0) & (rel <= n)
        group = ATT_GROUPS[d]

        def body(it, carry, d=d, shift=shift, nch=nch, cr=cr, band=band, group=group):
            def gather(ref, starts):
                parts = [ref[pl.ds(pl.multiple_of(st, SUBLANES), cr), :] for st in starts]
                return parts[0] if len(parts) == 1 else jnp.concatenate(parts, axis=0)

            loaded = []
            for g in range(group):
                idx = it * group + g
                m = idx >> shift
                r = idx & (d - 1)
                first = m == 0
                pbase = jnp.where(first, 0, ta) + jnp.where(first, nch - 1, m - 1) * cr
                q_st = [(d * c + r) * n + m * cr for c in range(nch)]
                k_st = [pbase + (d * c + r) * n for c in range(nch)] + [ta + st for st in q_st]
                q_r = gather(qbuf, q_st)
                k_r = gather(kbuf, k_st).astype(bf16)
                v_r = jnp.concatenate([gather(vbuf, k_st).astype(bf16), ones], axis=-1)
                old = None
                if d != DILATIONS[0]:
                    old = (gather(acc_o, q_st), gather(acc_m, q_st), gather(acc_l, q_st))
                valid = band & (cur | ((tile > 0) | (m > 0)))
                loaded.append((q_st, q_r, k_r, v_r, old, valid))
            merged = []
            for q_st, q_r, k_r, v_r, old, valid in loaded:
                o_h, m_h, l_h = [], [], []
                for hh in range(2):
                    qm = jnp.where(left if hh == 0 else ~left, q_r, 0.0).astype(bf16)
                    s = jnp.where(valid, _dot_nt(qm, k_r), -1e30)
                    mx = jnp.max(s, axis=-1, keepdims=True)
                    pv = _dot(jnp.exp(s - mx).astype(bf16), v_r)
                    m_h.append(mx)
                    o_h.append(pv[:, :LANES])
                    l_h.append(pv[:, LANES:])
                o_new = jnp.where(left, o_h[0], o_h[1])
                m_new = jnp.where(left, m_h[0], m_h[1])
                l_new = jnp.where(left, l_h[0], l_h[1])
                if old is not None:
                    o_old, m_old, l_old = old
                    m_tot = jnp.maximum(m_old, m_new)
                    a = jnp.exp(m_old - m_tot)
                    b = jnp.exp(m_new - m_tot)
                    o_new = a * o_old + b * o_new
                    l_new = a * l_old + b * l_new
                    m_new = m_tot
                merged.append((q_st, o_new, m_new, l_new))
            for q_st, o_new, m_new, l_new in merged:
                for c, st in enumerate(q_st):
                    dst = pl.ds(pl.multiple_of(st, SUBLANES), cr)
                    acc_o[dst, :] = o_new[c * cr:(c + 1) * cr, :]
                    acc_m[dst, :] = m_new[c * cr:(c + 1) * cr, :]
                    acc_l[dst, :] = l_new[c * cr:(c + 1) * cr, :]
            return carry

        lax.fori_loop(0, ta // n // group, body, 0)

    for r in range(dmax):
        o_ref[pl.ds(r, n, stride=dmax), :] = acc_o[r * n:(r + 1) * n, :] / acc_l[r * n:(r + 1) * n, :]
    kbuf[0:ta, :] = kbuf[ta:2 * ta, :]
    vbuf[0:ta, :] = vbuf[ta:2 * ta, :]


def _dilated_attention(proj, bsz, seq, q_norm, k_norm):
    t = bsz * seq
    ta = ATT_TILE
    nt = seq // ta
    pairs = ATT_HEADS // 2
    bsum = np.zeros((LANES, LANES), np.float32)
    for h in range(2):
        bsum[h * ATT_HEAD_DIM:(h + 1) * ATT_HEAD_DIM, h * ATT_HEAD_DIM:(h + 1) * ATT_HEAD_DIM] = 1.0 / ATT_HEAD_DIM
    blk = pl.BlockSpec((ta, LANES), lambda b, p, i: (b * nt + i, p))
    const = lambda b, p, i: (0, 0)
    vec = pl.BlockSpec((1, LANES), const)
    buf = lambda rows: pltpu.VMEM((rows, LANES), f32)
    return pl.pallas_call(
        _attn_kernel, grid=(bsz, pairs, nt),
        in_specs=[blk, blk, blk, vec, vec, pl.BlockSpec((LANES, LANES), const)],
        out_specs=blk,
        out_shape=jax.ShapeDtypeStruct((t, GROUP_WIDTH), f32),
        scratch_shapes=[buf(ta), buf(2 * ta), buf(2 * ta), buf(ta), buf(ta), buf(ta)],
        compiler_params=_params(3), name="dilated_attention",
    )(proj["q"], proj["k"], proj["v"], jnp.tile(q_norm, 2).reshape(1, LANES), jnp.tile(k_norm, 2).reshape(1, LANES),
      jnp.asarray(bsum, bf16))


def _memkv_kernel(mem_ref, gm_ref, wkv_ref, gk_ref, k_ref, v_ref):
    hm = _rms(mem_ref[...], gm_ref[...]).astype(bf16)
    kv = _dot(hm, wkv_ref[...])
    gk = gk_ref[...]
    for hh in range(CA_HEADS):
        sl = slice(hh * CA_HEAD_DIM, (hh + 1) * CA_HEAD_DIM)
        k_ref[:, sl] = _rms(kv[:, sl], gk).astype(bf16)
    v_ref[...] = kv[:, D_MODEL:].astype(bf16)


def _memory_kv(mem, norm_mem, wkv_b, k_norm):
    bsz, m, _ = mem.shape
    const = lambda b: (0, 0)
    blk = pl.BlockSpec((None, m, D_MODEL), lambda b: (b, 0, 0))
    return pl.pallas_call(
        _memkv_kernel, grid=(bsz,),
        in_specs=[blk, pl.BlockSpec((1, D_MODEL), const), pl.BlockSpec((D_MODEL, 2 * D_MODEL), const),
                  pl.BlockSpec((1, CA_HEAD_DIM), const)],
        out_specs=[blk, blk],
        out_shape=[jax.ShapeDtypeStruct((bsz, m, D_MODEL), bf16)] * 2,
        compiler_params=_params(1), name="memory_kv",
    )(mem, norm_mem.reshape(1, -1), wkv_b, k_norm.reshape(1, -1))


def _outca_kernel(x_ref, y0_ref, y1_ref, y2_ref, y3_ref, w0_ref, w1_ref, w2_ref, w3_ref, gca_ref, wq_ref, kn_ref,
                  v_ref, gq_ref, wo_ref, o_ref):
    x1 = x_ref[...]
    for y_ref, w_ref in ((y0_ref, w0_ref), (y1_ref, w1_ref), (y2_ref, w2_ref), (y3_ref, w3_ref)):
        x1 = x1 + _dot(y_ref[...].astype(bf16), w_ref[...])
    h = _rms(x1, gca_ref[...]).astype(bf16)
    q = _dot(h, wq_ref[...])
    gq = gq_ref[...]
    outs = []
    for hh in range(CA_HEADS):
        sl = slice(hh * CA_HEAD_DIM, (hh + 1) * CA_HEAD_DIM)
        qn = (_rms(q[:, sl], gq) * (CA_HEAD_DIM ** -0.5)).astype(bf16)
        s = _dot_nt(qn, kn_ref[:, sl])
        m = jnp.max(s, axis=-1, keepdims=True)
        p = jnp.exp(s - m)
        l = jnp.sum(p, axis=-1, keepdims=True)
        outs.append((_dot(p.astype(bf16), v_ref[:, sl]) / l).astype(bf16))
    o_ref[...] = x1 + _dot(jnp.concatenate(outs, axis=-1), wo_ref[...])


def _outproj_cross_attention(x2d, ys, w_out_b, bsz, seq, norm_ca, wq_b, kn, vv, q_norm, wo_b):
    t = bsz * seq
    tm = TM_CA
    ns = seq // tm
    m = kn.shape[1]
    row = lambda b, i: (b * ns + i, 0)
    const = lambda b, i: (0, 0)
    mix = pl.BlockSpec((tm, GROUP_WIDTH), row)
    wblk = pl.BlockSpec((GROUP_WIDTH, D_MODEL), const)
    sq = pl.BlockSpec((D_MODEL, D_MODEL), const)
    kvb = pl.BlockSpec((None, m, D_MODEL), lambda b, i: (b, 0, 0))
    w_parts = [w_out_b[j * GROUP_WIDTH:(j + 1) * GROUP_WIDTH] for j in range(4)]
    return pl.pallas_call(
        _outca_kernel, grid=(bsz, ns),
        in_specs=[pl.BlockSpec((tm, D_MODEL), row), mix, mix, mix, mix, wblk, wblk, wblk, wblk,
                  pl.BlockSpec((1, D_MODEL), const), sq, kvb, kvb, pl.BlockSpec((1, CA_HEAD_DIM), const), sq],
        out_specs=pl.BlockSpec((tm, D_MODEL), row),
        out_shape=jax.ShapeDtypeStruct((t, D_MODEL), f32),
        compiler_params=_params(2), name="outproj_cross_attention",
    )(x2d, *ys, *w_parts, norm_ca.reshape(1, -1), wq_b, kn, vv, q_norm.reshape(1, -1), wo_b)


def _to_token_tiles(ref, x):
    rows = x.shape[0]
    for s in range(TOKEN_TILE_ROWS):
        ref[pl.ds(s, rows, stride=TOKEN_TILE_ROWS), :] = x[:, s * LANES:(s + 1) * LANES]


def _from_token_tiles(ref, rows):
    return jnp.concatenate([ref[pl.ds(s, rows, stride=TOKEN_TILE_ROWS), :] for s in range(TOKEN_TILE_ROWS)], axis=-1)


def _router_kernel(x_ref, g_ref, w1_ref, w2_ref, b_ref, slt_ref, h_ref, meta_ref, g1_ref, g2_ref, cnt_ref, base):
    @pl.when(pl.program_id(0) == 0)
    def _():
        base[...] = jnp.zeros_like(base)

    tm = TM_ROUTER
    h = _rms(x_ref[...], g_ref[...])
    _to_token_tiles(h_ref, h)
    h1, h2 = _split2(h)
    w1 = w1_ref[...]
    logits = (_dot(h1, w1) + _dot(h1, w2_ref[...]) + _dot(h2, w1)) + b_ref[...]
    lane = lax.broadcasted_iota(jnp.int32, (tm, LANES), 1)
    lanef = lane.astype(f32)
    neg = -jnp.inf

    lgm = jnp.where((lane >= N_EXPERTS) & (lane < N_EXPERTS + N_EXPERT_GROUPS), logits, neg)
    mg = jnp.max(lgm, axis=-1, keepdims=True)
    g_p = 1.0 / jnp.sum(jnp.exp(lgm - mg), axis=-1, keepdims=True)
    gi = jnp.min(jnp.where(lgm == mg, lanef, float(LANES)), axis=-1, keepdims=True) - float(N_EXPERTS)

    grp = (lane >> (EXPERTS_PER_GROUP.bit_length() - 1)).astype(f32)
    lem = jnp.where((grp == gi) & (lane < N_EXPERTS), logits, neg)
    me = jnp.max(lem, axis=-1, keepdims=True)
    i1 = jnp.min(jnp.where(lem == me, lanef, float(LANES)), axis=-1, keepdims=True)
    le2 = jnp.where(lanef == i1, neg, lem)
    m2 = jnp.max(le2, axis=-1, keepdims=True)
    i2 = jnp.min(jnp.where(le2 == m2, lanef, float(LANES)), axis=-1, keepdims=True)
    e2 = jnp.exp(m2 - me)
    gate1 = g_p / (1.0 + e2)
    gate2 = g_p * e2 / (1.0 + e2)

    oh1 = (lanef == i1).astype(f32)
    oh2 = (lanef == i2).astype(f32)
    cnt = oh1 + oh2
    before = _dot(slt_ref[...], cnt.astype(bf16)) + base[...]
    r1 = jnp.sum(oh1 * before, axis=-1, keepdims=True)
    r2 = jnp.sum(oh2 * before, axis=-1, keepdims=True)
    total = base[...] + jnp.sum(cnt, axis=0, keepdims=True)
    base[...] = total
    cnt_ref[...] = total

    packed = jnp.where(lane == 0, i1, jnp.where(lane == 1, i2, jnp.where(lane == 2, r1, r2)))
    meta_ref[...] = packed.T[0:SUBLANES, :].astype(jnp.int32)
    g1_ref[...] = jnp.broadcast_to(gate1, (tm, LANES))
    g2_ref[...] = jnp.broadcast_to(gate2, (tm, LANES))


def _router(x2d, norm_g, wg, bg, we, be):
    t = x2d.shape[0]
    tm = TM_ROUTER
    pad = LANES - N_EXPERTS - N_EXPERT_GROUPS
    wr = jnp.pad(jnp.concatenate([we, wg], axis=1), ((0, 0), (0, pad)))
    br = jnp.pad(jnp.concatenate([be, bg]), (0, pad)).reshape(1, LANES)
    w1, w2 = _split2(wr)
    slt = jnp.asarray(np.tril(np.ones((tm, tm), np.float32), -1), bf16)
    const = lambda i: (0, 0)
    row = lambda i: (i, 0)
    wspec = pl.BlockSpec((D_MODEL, LANES), const)
    gate = pl.BlockSpec((tm, LANES), row)
    return pl.pallas_call(
        _router_kernel, grid=(t // tm,),
        in_specs=[pl.BlockSpec((tm, D_MODEL), row), pl.BlockSpec((1, D_MODEL), const), wspec, wspec,
                  pl.BlockSpec((1, LANES), const), pl.BlockSpec((tm, tm), const)],
        out_specs=[pl.BlockSpec((tm * TOKEN_TILE_ROWS, LANES), row), pl.BlockSpec((SUBLANES, tm), lambda i: (0, i)),
                   gate, gate, pl.BlockSpec((1, LANES), const)],
        out_shape=[jax.ShapeDtypeStruct((t * TOKEN_TILE_ROWS, LANES), f32), jax.ShapeDtypeStruct((SUBLANES, t), jnp.int32),
                   jax.ShapeDtypeStruct((t, LANES), f32), jax.ShapeDtypeStruct((t, LANES), f32),
                   jax.ShapeDtypeStruct((1, LANES), f32)],
        scratch_shapes=[pltpu.VMEM((1, LANES), f32)],
        compiler_params=_params(1), name="moe_router",
    )(x2d, norm_g.reshape(1, -1), w1, w2, br, slt)


def _tile_copy(src, src_tile, dst, dst_tile, sem):
    rows = TOKEN_TILE_ROWS
    return pltpu.make_async_copy(src.at[pl.ds(pl.multiple_of(src_tile * rows, rows), rows)],
                                 dst.at[pl.ds(pl.multiple_of(dst_tile * rows, rows), rows)], sem)


def _dispatch_kernel(starts_ref, e1_ref, e2_ref, r1_ref, r2_ref, h_ref, init_ref, buf_ref, sem):
    del init_ref

    def issue(j, carry):
        _tile_copy(h_ref, j, buf_ref, starts_ref[e1_ref[j]] + r1_ref[j], sem).start(priority=0)
        _tile_copy(h_ref, j, buf_ref, starts_ref[e2_ref[j]] + r2_ref[j], sem).start(priority=1)
        return carry

    lax.fori_loop(0, TD_DISPATCH, issue, 0, unroll=DMA_UNROLL)

    def drain(j, carry):
        _tile_copy(h_ref, 0, buf_ref, 0, sem).wait()
        _tile_copy(h_ref, 0, buf_ref, 0, sem).wait()
        return carry

    lax.fori_loop(0, TD_DISPATCH, drain, 0, unroll=DMA_UNROLL)


def _dispatch(h_tiles, meta, starts_pad, init_buf):
    t = meta.shape[1]
    td = TD_DISPATCH
    smem = pl.BlockSpec((td,), lambda i, s: (i,), memory_space=pltpu.SMEM)
    anyspec = pl.BlockSpec(memory_space=pl.ANY)
    hspec = pl.BlockSpec((td * TOKEN_TILE_ROWS, LANES), lambda i, s: (i, 0))
    grid_spec = pltpu.PrefetchScalarGridSpec(
        num_scalar_prefetch=1, grid=(t // td,), in_specs=[smem, smem, smem, smem, hspec, anyspec], out_specs=anyspec,
        scratch_shapes=[pltpu.SemaphoreType.DMA(())])
    return pl.pallas_call(
        _dispatch_kernel, grid_spec=grid_spec, out_shape=jax.ShapeDtypeStruct(init_buf.shape, f32),
        input_output_aliases={6: 0}, compiler_params=_params(1), name="moe_dispatch",
    )(starts_pad, meta[0], meta[1], meta[2], meta[3], h_tiles, init_buf)


def _expert_kernel(blk_e_ref, nused_ref, x_ref, wg_ref, wu_ref, wd_ref, o_ref):
    del blk_e_ref
    used = pl.program_id(0) < nused_ref[0]

    @pl.when(used)
    def _():
        x = _from_token_tiles(x_ref, MOE_BLOCK).astype(bf16)
        g = _dot(x, wg_ref[...].astype(bf16))
        u = _dot(x, wu_ref[...].astype(bf16))
        _to_token_tiles(o_ref, _dot((_silu(g) * u).astype(bf16), wd_ref[...].astype(bf16)))

    @pl.when(jnp.logical_not(used))
    def _():
        o_ref[...] = jnp.zeros_like(o_ref)


def _experts(buf, blk_e, n_used, layer, w_gate, w_up, w_down):
    n_blocks = buf.shape[0] // (MOE_BLOCK * TOKEN_TILE_ROWS)
    xspec = pl.BlockSpec((MOE_BLOCK * TOKEN_TILE_ROWS, LANES), lambda i, be, nu: (i, 0))
    wspec = lambda a, b: pl.BlockSpec((None, None, a, b), lambda i, be, nu: (layer, be[i], 0, 0))
    grid_spec = pltpu.PrefetchScalarGridSpec(
        num_scalar_prefetch=2, grid=(n_blocks,),
        in_specs=[xspec, wspec(D_MODEL, EXPERT_HIDDEN), wspec(D_MODEL, EXPERT_HIDDEN), wspec(EXPERT_HIDDEN, D_MODEL)],
        out_specs=xspec)
    return pl.pallas_call(
        _expert_kernel, grid_spec=grid_spec, out_shape=jax.ShapeDtypeStruct(buf.shape, f32),
        compiler_params=_params(1), name="moe_experts",
    )(blk_e, n_used, buf, w_gate, w_up, w_down)


def _combine_kernel(starts_ref, e1_ref, e2_ref, r1_ref, r2_ref, x_ref, g1_ref, g2_ref, yb_ref, o_ref, rows0, rows1, sem):
    tc = TC_COMBINE

    def issue(j, carry):
        _tile_copy(yb_ref, starts_ref[e1_ref[j]] + r1_ref[j], rows0, j, sem).start(priority=0)
        _tile_copy(yb_ref, starts_ref[e2_ref[j]] + r2_ref[j], rows1, j, sem).start(priority=1)
        return carry

    lax.fori_loop(0, tc, issue, 0, unroll=DMA_UNROLL)

    def drain(j, carry):
        _tile_copy(yb_ref, 0, rows0, 0, sem).wait()
        _tile_copy(yb_ref, 0, rows1, 0, sem).wait()
        return carry

    lax.fori_loop(0, tc, drain, 0, unroll=DMA_UNROLL)
    reps = D_MODEL // LANES
    g1 = jnp.concatenate([g1_ref[...]] * reps, axis=-1)
    g2 = jnp.concatenate([g2_ref[...]] * reps, axis=-1)
    o_ref[...] = x_ref[...] + (g1 * _from_token_tiles(rows0, tc) + g2 * _from_token_tiles(rows1, tc))


def _combine(x2d, g1, g2, yb, meta, starts_pad):
    t = x2d.shape[0]
    tc = TC_COMBINE
    smem = pl.BlockSpec((tc,), lambda i, s: (i,), memory_space=pltpu.SMEM)
    row = lambda i, s: (i, 0)
    tiles = pltpu.VMEM((tc * TOKEN_TILE_ROWS, LANES), f32)
    grid_spec = pltpu.PrefetchScalarGridSpec(
        num_scalar_prefetch=1, grid=(t // tc,),
        in_specs=[smem, smem, smem, smem, pl.BlockSpec((tc, D_MODEL), row), pl.BlockSpec((tc, LANES), row),
                  pl.BlockSpec((tc, LANES), row), pl.BlockSpec(memory_space=pl.ANY)],
        out_specs=pl.BlockSpec((tc, D_MODEL), row),
        scratch_shapes=[tiles, tiles, pltpu.SemaphoreType.DMA(())])
    return pl.pallas_call(
        _combine_kernel, grid_spec=grid_spec, out_shape=jax.ShapeDtypeStruct((t, D_MODEL), f32),
        compiler_params=_params(1), name="moe_combine",
    )(starts_pad, meta[0], meta[1], meta[2], meta[3], x2d, g1, g2, yb)


def _moe(x2d, norm_g, wg, bg, we, be, layer, w_gate, w_up, w_down, init_buf):
    t = x2d.shape[0]
    h_tiles, meta, g1, g2, cnt = _router(x2d, norm_g, wg, bg, we, be)
    counts = cnt[0, :N_EXPERTS].astype(jnp.int32)
    padded = ((counts + MOE_BLOCK - 1) // MOE_BLOCK) * MOE_BLOCK
    ends_pad = jnp.cumsum(padded)
    starts_pad = (ends_pad - padded).astype(jnp.int32)
    a = t * TOP_K
    p_rows = (-(-a // MOE_BLOCK)) * MOE_BLOCK + N_EXPERTS * MOE_BLOCK
    n_blocks = p_rows // MOE_BLOCK
    blk_start = jnp.arange(n_blocks, dtype=jnp.int32) * MOE_BLOCK
    blk_e = jnp.minimum(jnp.sum((ends_pad[None, :] <= blk_start[:, None]).astype(jnp.int32), axis=1), N_EXPERTS - 1)
    n_used = (ends_pad[-1:] // MOE_BLOCK).astype(jnp.int32)
    if init_buf is None:
        init_buf = jnp.zeros((p_rows * TOKEN_TILE_ROWS, LANES), f32)
    buf = _dispatch(h_tiles, meta, starts_pad, init_buf)
    yb = _experts(buf, blk_e, n_used, layer, w_gate, w_up, w_down)
    return _combine(x2d, g1, g2, yb, meta, starts_pad), buf


def _prep_w_in_kernel(w_ref, o_ref):
    cut = GROUP_WIDTH + SSD_CONV_CH
    o_ref[:, 0:cut] = w_ref[:, 0:cut].astype(bf16)
    lane = lax.broadcasted_iota(jnp.int32, (W_PREP_ROWS, LANES), 1)
    o_ref[:, cut:cut + LANES] = jnp.where(lane < SSD_HEADS, w_ref[:, cut:cut + LANES], 0.0).astype(bf16)
    tail = w_ref[:, cut:N_IN_RAW]
    o_ref[:, cut + LANES:N_PROJ] = tail[:, SSD_HEADS:SSD_HEADS + N_PROJ - cut - LANES].astype(bf16)


def _prep_w_in(w_in):
    depth = w_in.shape[0]
    rows = W_PREP_ROWS
    return pl.pallas_call(
        _prep_w_in_kernel, grid=(depth, D_MODEL // rows),
        in_specs=[pl.BlockSpec((None, rows, N_IN_RAW), lambda l, i: (l, i, 0))],
        out_specs=pl.BlockSpec((None, rows, N_PROJ), lambda l, i: (l, i, 0)),
        out_shape=jax.ShapeDtypeStruct((depth, D_MODEL, N_PROJ), bf16),
        compiler_params=_params(2), name="prep_w_in",
    )(w_in)


def kernel(x, mem, norm_mix, w_in, ssd_conv_w, ssd_conv_b, ssd_dt_bias, ssd_a_log, ssd_d, ssd_norm, sc_conv_w,
           attn_q_norm, attn_k_norm, cf_conv_w, cf_conv_b, cf_ln_g, cf_ln_b, w_out, norm_ca, norm_mem, ca_wq, ca_wkv,
           ca_q_norm, ca_k_norm, ca_wo, norm_ffn, router_group_w, router_group_b, router_expert_w, router_expert_b,
           exp_w_gate, exp_w_up, exp_w_down):
    bsz, seq, d = x.shape
    depth = w_in.shape[0]
    assert d == D_MODEL and seq % ATT_TILE == 0 and seq % TS_SSD == 0 and (bsz * seq) % TM_PROJ == 0
    x2d = x.reshape(bsz * seq, d)
    w_cat = _prep_w_in(w_in)
    moe_buf = None
    for l in range(depth):
        proj = _in_projection(x2d, norm_mix[l].reshape(1, -1), l, w_cat)
        y_ssd = _ssd(proj, bsz, seq, ssd_conv_w[l], ssd_conv_b[l], ssd_dt_bias[l], ssd_a_log[l], ssd_d[l], ssd_norm[l])
        y_sc, y_cf = _convs(proj, bsz, seq, sc_conv_w[l], cf_conv_w[l], cf_conv_b[l], cf_ln_g[l], cf_ln_b[l])
        y_att = _dilated_attention(proj, bsz, seq, attn_q_norm[l], attn_k_norm[l])
        kn, vv = _memory_kv(mem, norm_mem[l], ca_wkv[l].astype(bf16), ca_k_norm[l])
        x2d = _outproj_cross_attention(x2d, (y_ssd, y_sc, y_att, y_cf), w_out[l].astype(bf16), bsz, seq, norm_ca[l],
                                       ca_wq[l].astype(bf16), kn, vv, ca_q_norm[l], ca_wo[l].astype(bf16))
        x2d, moe_buf = _moe(x2d, norm_ffn[l], router_group_w[l], router_group_b[l], router_expert_w[l],
                            router_expert_b[l], l, exp_w_gate, exp_w_up, exp_w_down, moe_buf)
    return x2d.reshape(bsz, seq, d)
```

```python
import functools

import numpy as np
import jax
import jax.numpy as jnp
from jax import lax
from jax.experimental import pallas as pl
from jax.experimental.pallas import tpu as pltpu

f32 = jnp.float32
bf16 = jnp.bfloat16

D_MODEL = 1024
GROUP_WIDTH = 384
SSD_HEADS = 6
SSD_HEAD_DIM = 64
SSD_GROUPS = 2
SSD_STATE = 128
SSD_CONV = 4
SSD_CHUNK = 128
SSD_CONV_CH = GROUP_WIDTH + 2 * SSD_GROUPS * SSD_STATE
SC_WIDTH = 3
ATT_HEADS = 6
ATT_HEAD_DIM = 64
ATT_BLOCK = 128
DILATIONS = (1, 4, 16)
ATT_TILE = ATT_BLOCK * max(DILATIONS)
ATT_GROUPS = {1: 8, 4: 8, 16: 8}
CF_KERNEL = 31
CA_HEADS = 4
CA_HEAD_DIM = D_MODEL // CA_HEADS
N_EXPERT_GROUPS = 4
EXPERTS_PER_GROUP = 8
N_EXPERTS = N_EXPERT_GROUPS * EXPERTS_PER_GROUP
TOP_K = 2
EXPERT_HIDDEN = D_MODEL // 2
MOE_BLOCK = 512
RMS_EPS = 1e-6
LN_EPS = 1e-5

LANES = 128
SUBLANES = 8
VMEM_LIMIT_BYTES = 56 * 1024 * 1024

TM_PROJ = 512
TS_SSD = 1024
SSD_UNROLL = 8
TS_CONV = 512
CONV_ROWS = 64
TM_CA = 512
TM_ROUTER = 512
TD_DISPATCH = 1024
TC_COMBINE = 1024
TOKEN_TILE_ROWS = 8
DMA_UNROLL = 8

_SEGS = (("z", GROUP_WIDTH), ("xbc", SSD_CONV_CH), ("dt", LANES), ("sc_b", GROUP_WIDTH), ("sc_c", GROUP_WIDTH),
         ("sc_h", GROUP_WIDTH), ("q", GROUP_WIDTH), ("k", GROUP_WIDTH), ("v", GROUP_WIDTH), ("cf_a", GROUP_WIDTH),
         ("cf_g", GROUP_WIDTH))
N_PROJ = sum(w for _, w in _SEGS)
N_IN_RAW = N_PROJ - LANES + SSD_HEADS
W_PREP_ROWS = 256


def _params(n_axes):
    return pltpu.CompilerParams(dimension_semantics=("arbitrary",) * n_axes, vmem_limit_bytes=VMEM_LIMIT_BYTES)


def _dot(a, b):
    return jnp.dot(a, b, preferred_element_type=f32)


def _dot_nt(a, b):
    return lax.dot_general(a, b, (((1,), (1,)), ((), ())), preferred_element_type=f32)


def _split2(x):
    hi = x.astype(bf16)
    lo = (x - hi.astype(f32)).astype(bf16)
    return hi, lo


def _split3(x):
    hi = x.astype(bf16)
    r = x - hi.astype(f32)
    mid = r.astype(bf16)
    lo = (r - mid.astype(f32)).astype(bf16)
    return hi, mid, lo


def _dot_exact_rhs(x, m_bf16, n_split):
    parts = _split3(x) if n_split == 3 else _split2(x)
    acc = _dot(parts[0], m_bf16)
    for p in parts[1:]:
        acc = acc + _dot(p, m_bf16)
    return acc


def _sigmoid(x):
    return 1.0 / (1.0 + jnp.exp(-x))


def _silu(x):
    return x * _sigmoid(x)


def _rms(x, gain):
    ms = jnp.mean(x * x, axis=-1, keepdims=True)
    return x * lax.rsqrt(ms + RMS_EPS) * gain


def _inproj_kernel(x_ref, g_ref, w_ref, *o_refs):
    h = _rms(x_ref[...], g_ref[...]).astype(bf16)
    off = 0
    for (_, w), o_ref in zip(_SEGS, o_refs):
        o_ref[...] = _dot(h, w_ref[:, off:off + w])
        off += w


def _in_projection(x2d, gain, layer, w_cat):
    t = x2d.shape[0]
    tm = TM_PROJ
    in_specs = [pl.BlockSpec((tm, D_MODEL), lambda i: (i, 0)), pl.BlockSpec((1, D_MODEL), lambda i: (0, 0)),
                pl.BlockSpec((None, D_MODEL, N_PROJ), lambda i: (layer, 0, 0))]
    out_specs = [pl.BlockSpec((tm, w), lambda i: (i, 0)) for _, w in _SEGS]
    out_shape = [jax.ShapeDtypeStruct((t, w), f32) for _, w in _SEGS]
    outs = pl.pallas_call(
        _inproj_kernel, grid=(t // tm,), in_specs=in_specs, out_specs=out_specs, out_shape=out_shape,
        compiler_params=_params(1), name="in_projection",
    )(x2d, gain, w_cat)
    return dict(zip([n for n, _ in _SEGS], outs))


def _ssd_kernel(xbc_ref, dt_ref, z_ref, cw_ref, cb_ref, dtb_ref, alog_ref, dvec_ref, ng_ref, tril_ref, e384_ref,
                e768_ref, o_ref, xbuf, xc, state):
    ts = TS_SSD
    q = SSD_CHUNK
    halo = SUBLANES

    @pl.when(pl.program_id(1) == 0)
    def _():
        xbuf[0:halo, :] = jnp.zeros((halo, SSD_CONV_CH), f32)
        state[...] = jnp.zeros_like(state)

    xbuf[halo:halo + ts, :] = xbc_ref[...]
    acc = cb_ref[...] + cw_ref[0:1, :] * xbuf[halo - 3:halo - 3 + ts, :]
    for k in range(1, SSD_CONV):
        acc = acc + cw_ref[k:k + 1, :] * xbuf[halo - 3 + k:halo - 3 + k + ts, :]
    xc[...] = _silu(acc)
    xbuf[0:halo, :] = xbuf[ts:ts + halo, :]

    lane_row = lax.broadcasted_iota(jnp.int32, (1, LANES), 1)
    head_lane = lane_row < SSD_HEADS
    a_row = jnp.where(head_lane, -jnp.exp(alog_ref[...]), 0.0)
    lane_sq = lax.broadcasted_iota(jnp.int32, (q, LANES), 1)
    row_sq = lax.broadcasted_iota(jnp.int32, (q, LANES), 0)
    causal = row_sq >= lane_sq
    left = lane_sq < SSD_HEAD_DIM
    tril = tril_ref[...]
    e384 = e384_ref[...]
    e768 = e768_ref[...]
    dvec = dvec_ref[...]
    ng = ng_ref[...]

    def chunk(c, carry):
        r0 = pl.multiple_of(c * q, q)
        dtr = dt_ref[pl.ds(r0, q), :] + dtb_ref[...]
        sp = jnp.maximum(dtr, 0.0) + jnp.log1p(jnp.exp(-jnp.abs(dtr)))
        dt = jnp.where(head_lane, sp, 0.0)
        da = dt * a_row
        d3 = _split3(da)
        cs = _dot(tril, d3[0]) + _dot(tril, d3[1]) + _dot(tril, d3[2])
        cs_t = cs.T
        c3 = _split3(cs)
        cs384 = _dot(c3[0], e384) + _dot(c3[1], e384) + _dot(c3[2], e384)
        cs768 = _dot(c3[0], e768) + _dot(c3[1], e768) + _dot(c3[2], e768)
        dt384 = _dot_exact_rhs(dt, e384, 2)
        last384 = cs384[q - 1:q, :]
        decay384 = jnp.exp(last384 - cs384)
        expcs384 = jnp.exp(cs384)

        xs = xc[pl.ds(r0, q), 0:GROUP_WIDTH]
        bm = xc[pl.ds(r0, q), GROUP_WIDTH:GROUP_WIDTH + SSD_GROUPS * SSD_STATE]
        cm = xc[pl.ds(r0, q), GROUP_WIDTH + SSD_GROUPS * SSD_STATE:SSD_CONV_CH]
        xdt = xs * dt384
        xdt_b = xdt.astype(bf16)
        xdec_b = (xdt * decay384).astype(bf16)
        st = state[...]
        st_b = st.astype(bf16)

        cb_g, bt_g, c_g = [], [], []
        for g in range(SSD_GROUPS):
            b_f = bm[:, g * SSD_STATE:(g + 1) * SSD_STATE]
            c_b = cm[:, g * SSD_STATE:(g + 1) * SSD_STATE].astype(bf16)
            cb_g.append(_dot_nt(c_b, b_f.astype(bf16)))
            bt_g.append(b_f.T.astype(bf16))
            c_g.append(c_b)

        y_parts, st_parts = [], []
        for p in range(SSD_HEADS // 2):
            sl = slice(p * LANES, (p + 1) * LANES)
            xdt_p, xdec_p, st_p = xdt_b[:, sl], xdec_b[:, sl], st_b[:, sl]
            yd, s_new, yo = [], {}, {}
            for hh in range(2):
                h = 2 * p + hh
                g = h // (SSD_HEADS // SSD_GROUPS)
                diff = cs768[:, h * LANES:(h + 1) * LANES] - cs_t[h:h + 1, :]
                seg = jnp.exp(jnp.where(causal, diff, -1e30))
                m = (cb_g[g] * seg).astype(bf16)
                yd.append(_dot(m, xdt_p))
                if g not in s_new:
                    s_new[g] = _dot(bt_g[g], xdec_p)
                    yo[g] = _dot(c_g[g], st_p)
            g0 = (2 * p) // (SSD_HEADS // SSD_GROUPS)
            g1 = (2 * p + 1) // (SSD_HEADS // SSD_GROUPS)
            y_diag = jnp.where(left, yd[0], yd[1])
            y_off = jnp.where(left, yo[g0], yo[g1])
            s_pair = jnp.where(left, s_new[g0], s_new[g1])
            y_parts.append(y_diag + y_off * expcs384[:, sl] + xs[:, sl] * dvec[:, sl])
            st_parts.append(st[:, sl] * expcs384[q - 1:q, sl] + s_pair)
        state[...] = jnp.concatenate(st_parts, axis=-1)
        y = jnp.concatenate(y_parts, axis=-1)
        yg = y * _silu(z_ref[pl.ds(r0, q), :])
        o_ref[pl.ds(r0, q), :] = _rms(yg, ng)
        return carry

    lax.fori_loop(0, ts // q, chunk, 0, unroll=SSD_UNROLL)


def _ssd_constants():
    q = SSD_CHUNK
    tril = np.tril(np.ones((q, q), np.float32))
    e384 = np.zeros((LANES, GROUP_WIDTH), np.float32)
    e768 = np.zeros((LANES, SSD_HEADS * LANES), np.float32)
    for h in range(SSD_HEADS):
        e384[h, h * SSD_HEAD_DIM:(h + 1) * SSD_HEAD_DIM] = 1.0
        e768[h, h * LANES:(h + 1) * LANES] = 1.0
    return jnp.asarray(tril, bf16), jnp.asarray(e384, bf16), jnp.asarray(e768, bf16)


def _ssd(proj, bsz, seq, conv_w, conv_b, dt_bias, a_log, d_skip, norm_g):
    t = bsz * seq
    ts = TS_SSD
    ns = seq // ts
    tril, e384, e768 = _ssd_constants()
    pad = LANES - SSD_HEADS
    dtb = jnp.pad(dt_bias, (0, pad)).reshape(1, LANES)
    alog = jnp.pad(a_log, (0, pad)).reshape(1, LANES)
    dvec = jnp.repeat(d_skip, SSD_HEAD_DIM).reshape(1, GROUP_WIDTH)
    row = lambda b, i: (b * ns + i, 0)
    const = lambda b, i: (0, 0)
    return pl.pallas_call(
        _ssd_kernel, grid=(bsz, ns),
        in_specs=[pl.BlockSpec((ts, SSD_CONV_CH), row), pl.BlockSpec((ts, LANES), row), pl.BlockSpec((ts, GROUP_WIDTH), row),
                  pl.BlockSpec((SSD_CONV, SSD_CONV_CH), const), pl.BlockSpec((1, SSD_CONV_CH), const),
                  pl.BlockSpec((1, LANES), const), pl.BlockSpec((1, LANES), const), pl.BlockSpec((1, GROUP_WIDTH), const),
                  pl.BlockSpec((1, GROUP_WIDTH), const), pl.BlockSpec((SSD_CHUNK, SSD_CHUNK), const),
                  pl.BlockSpec((LANES, GROUP_WIDTH), const), pl.BlockSpec((LANES, SSD_HEADS * LANES), const)],
        out_specs=pl.BlockSpec((ts, GROUP_WIDTH), row),
        out_shape=jax.ShapeDtypeStruct((t, GROUP_WIDTH), f32),
        scratch_shapes=[pltpu.VMEM((ts + SUBLANES, SSD_CONV_CH), f32), pltpu.VMEM((ts, SSD_CONV_CH), f32),
                        pltpu.VMEM((SSD_STATE, GROUP_WIDTH), f32)],
        compiler_params=_params(2), name="ssd_scan",
    )(proj["xbc"], proj["dt"], proj["z"], conv_w, conv_b.reshape(1, -1), dtb, alog, dvec, norm_g.reshape(1, -1), tril,
      e384, e768)


_CF_HALO = 32


def _conv_kernel(scb_ref, scc_ref, sch_ref, cfa_ref, cfg_ref, scw_ref, cfw_ref, cfb_ref, lng_ref, lnb_ref, ysc_ref,
                 ycf_ref, ubuf, gbuf):
    ts = TS_CONV
    uh = SUBLANES

    @pl.when(pl.program_id(1) == 0)
    def _():
        ubuf[0:uh, :] = jnp.zeros((uh, GROUP_WIDTH), f32)
        gbuf[0:_CF_HALO, :] = jnp.zeros((_CF_HALO, GROUP_WIDTH), f32)

    ubuf[uh:uh + ts, :] = scc_ref[...] * sch_ref[...]
    gbuf[_CF_HALO:_CF_HALO + ts, :] = cfa_ref[...] * _sigmoid(cfg_ref[...])

    o = uh - (SC_WIDTH - 1)
    acc = scw_ref[0:1, :] * ubuf[o:o + ts, :]
    for k in range(1, SC_WIDTH):
        acc = acc + scw_ref[k:k + 1, :] * ubuf[o + k:o + k + ts, :]
    ysc_ref[...] = scb_ref[...] * acc

    o = _CF_HALO - (CF_KERNEL - 1)
    cfb = cfb_ref[...]
    lng = lng_ref[...]
    lnb = lnb_ref[...]
    for c in range(ts // CONV_ROWS):
        r0 = c * CONV_ROWS
        acc = cfb
        for s in range(SUBLANES):
            rows = CONV_ROWS + (SUBLANES if s else 0)
            part = None
            for k in range(CF_KERNEL):
                if (o + k) % SUBLANES != s:
                    continue
                a = r0 + o + k - s
                term = cfw_ref[k:k + 1, :] * gbuf[a:a + rows, :]
                part = term if part is None else part + term
            acc = acc + part[s:s + CONV_ROWS, :]
        mu = jnp.mean(acc, axis=-1, keepdims=True)
        xc = acc - mu
        var = jnp.mean(xc * xc, axis=-1, keepdims=True)
        y = xc * lax.rsqrt(var + LN_EPS) * lng + lnb
        ycf_ref[r0:r0 + CONV_ROWS, :] = _silu(y)

    ubuf[0:uh, :] = ubuf[ts:ts + uh, :]
    gbuf[0:_CF_HALO, :] = gbuf[ts:ts + _CF_HALO, :]


def _convs(proj, bsz, seq, sc_w, cf_w, cf_b, ln_g, ln_b):
    t = bsz * seq
    ts = TS_CONV
    ns = seq // ts
    row = lambda b, i: (b * ns + i, 0)
    const = lambda b, i: (0, 0)
    act = pl.BlockSpec((ts, GROUP_WIDTH), row)
    vec = pl.BlockSpec((1, GROUP_WIDTH), const)
    return pl.pallas_call(
        _conv_kernel, grid=(bsz, ns),
        in_specs=[act, act, act, act, act, pl.BlockSpec((SC_WIDTH, GROUP_WIDTH), const),
                  pl.BlockSpec((CF_KERNEL, GROUP_WIDTH), const), vec, vec, vec],
        out_specs=[act, act],
        out_shape=[jax.ShapeDtypeStruct((t, GROUP_WIDTH), f32)] * 2,
        scratch_shapes=[pltpu.VMEM((ts + SUBLANES, GROUP_WIDTH), f32), pltpu.VMEM((ts + _CF_HALO, GROUP_WIDTH), f32)],
        compiler_params=_params(2), name="gated_convs",
    )(proj["sc_b"], proj["sc_c"], proj["sc_h"], proj["cf_a"], proj["cf_g"], sc_w, cf_w, cf_b.reshape(1, -1),
      ln_g.reshape(1, -1), ln_b.reshape(1, -1))


def _attn_kernel(q_ref, k_ref, v_ref, qg_ref, kg_ref, bsum_ref, o_ref, qbuf, kbuf, vbuf, acc_o, acc_m, acc_l):
    ta = ATT_TILE
    n = ATT_BLOCK
    dmax = max(DILATIONS)
    tile = pl.program_id(2)

    @pl.when(tile == 0)
    def _():
        kbuf[0:ta, :] = jnp.zeros((ta, LANES), f32)
        vbuf[0:ta, :] = jnp.zeros((ta, LANES), f32)

    bsum = bsum_ref[...]

    def head_rms(x, g):
        ms = _dot_exact_rhs(x * x, bsum, 2)
        return x * lax.rsqrt(ms + RMS_EPS) * g

    qg = qg_ref[...] * (ATT_HEAD_DIM ** -0.5)
    kg = kg_ref[...]
    for r in range(dmax):
        res = pl.ds(r, n, stride=dmax)
        qbuf[r * n:(r + 1) * n, :] = head_rms(q_ref[res, :], qg)
        kbuf[ta + r * n:ta + (r + 1) * n, :] = head_rms(k_ref[res, :], kg)
        vbuf[ta + r * n:ta + (r + 1) * n, :] = v_ref[res, :]

    qa = lax.broadcasted_iota(jnp.int32, (n, 2 * n), 0)
    kb = lax.broadcasted_iota(jnp.int32, (n, 2 * n), 1)
    cur = kb >= n
    lane = lax.broadcasted_iota(jnp.int32, (n, LANES), 1)
    left = lane < ATT_HEAD_DIM
    ones = jnp.ones((2 * n, LANES), bf16)

    for d in DILATIONS:
        shift = d.bit_length() - 1
        nch = dmax // d
        cr = n // nch
        cshift = cr.bit_length() - 1
        lq = (qa & (cr - 1)) * nch + (qa >> cshift)
        kbl = kb & (n - 1)
        kpos = (kb & n) + (kbl & (cr - 1)) * nch + (kbl >> cshift)
        rel = lq + n - kpos
        band = (rel >= 0) & (rel <= n)
        group = ATT_GROUPS[d]

        def body(it, carry, d=d, shift=shift, nch=nch, cr=cr, band=band, group=group):
            def gather(ref, starts):
                parts = [ref[pl.ds(pl.multiple_of(st, SUBLANES), cr), :] for st in starts]
                return parts[0] if len(parts) == 1 else jnp.concatenate(parts, axis=0)

            loaded = []
            for g in range(group):
                idx = it * group + g
                m = idx >> shift
                r = idx & (d - 1)
                first = m == 0
                pbase = jnp.where(first, 0, ta) + jnp.where(first, nch - 1, m - 1) * cr
                q_st = [(d * c + r) * n + m * cr for c in range(nch)]
                k_st = [pbase + (d * c + r) * n for c in range(nch)] + [ta + st for st in q_st]
                q_r = gather(qbuf, q_st)
                k_r = gather(kbuf, k_st).astype(bf16)
                v_r = jnp.concatenate([gather(vbuf, k_st).astype(bf16), ones], axis=-1)
                old = None
                if d != DILATIONS[0]:
                    old = (gather(acc_o, q_st), gather(acc_m, q_st), gather(acc_l, q_st))
                valid = band & (cur | ((tile > 0) | (m > 0)))
                loaded.append((q_st, q_r, k_r, v_r, old, valid))
            merged = []
            for q_st, q_r, k_r, v_r, old, valid in loaded:
                o_h, m_h, l_h = [], [], []
                for hh in range(2):
                    qm = jnp.where(left if hh == 0 else ~left, q_r, 0.0).astype(bf16)
                    s = jnp.where(valid, _dot_nt(qm, k_r), -1e30)
                    mx = jnp.max(s, axis=-1, keepdims=True)
                    pv = _dot(jnp.exp(s - mx).astype(bf16), v_r)
                    m_h.append(mx)
                    o_h.append(pv[:, :LANES])
                    l_h.append(pv[:, LANES:])
                o_new = jnp.where(left, o_h[0], o_h[1])
                m_new = jnp.where(left, m_h[0], m_h[1])
                l_new = jnp.where(left, l_h[0], l_h[1])
                if old is not None:
                    o_old, m_old, l_old = old
                    m_tot = jnp.maximum(m_old, m_new)
                    a = jnp.exp(m_old - m_tot)
                    b = jnp.exp(m_new - m_tot)
                    o_new = a * o_old + b * o_new
                    l_new = a * l_old + b * l_new
                    m_new = m_tot
                merged.append((q_st, o_new, m_new, l_new))
            for q_st, o_new, m_new, l_new in merged:
                for c, st in enumerate(q_st):
                    dst = pl.ds(pl.multiple_of(st, SUBLANES), cr)
                    acc_o[dst, :] = o_new[c * cr:(c + 1) * cr, :]
                    acc_m[dst, :] = m_new[c * cr:(c + 1) * cr, :]
                    acc_l[dst, :] = l_new[c * cr:(c + 1) * cr, :]
            return carry

        for it in range(ta // n // group):
            body(it, 0)

    for r in range(dmax):
        o_ref[pl.ds(r, n, stride=dmax), :] = acc_o[r * n:(r + 1) * n, :] / acc_l[r * n:(r + 1) * n, :]
    kbuf[0:ta, :] = kbuf[ta:2 * ta, :]
    vbuf[0:ta, :] = vbuf[ta:2 * ta, :]


def _dilated_attention(proj, bsz, seq, q_norm, k_norm):
    t = bsz * seq
    ta = ATT_TILE
    nt = seq // ta
    pairs = ATT_HEADS // 2
    bsum = np.zeros((LANES, LANES), np.float32)
    for h in range(2):
        bsum[h * ATT_HEAD_DIM:(h + 1) * ATT_HEAD_DIM, h * ATT_HEAD_DIM:(h + 1) * ATT_HEAD_DIM] = 1.0 / ATT_HEAD_DIM
    blk = pl.BlockSpec((ta, LANES), lambda b, p, i: (b * nt + i, p))
    const = lambda b, p, i: (0, 0)
    vec = pl.BlockSpec((1, LANES), const)
    buf = lambda rows: pltpu.VMEM((rows, LANES), f32)
    return pl.pallas_call(
        _attn_kernel, grid=(bsz, pairs, nt),
        in_specs=[blk, blk, blk, vec, vec, pl.BlockSpec((LANES, LANES), const)],
        out_specs=blk,
        out_shape=jax.ShapeDtypeStruct((t, GROUP_WIDTH), f32),
        scratch_shapes=[buf(ta), buf(2 * ta), buf(2 * ta), buf(ta), buf(ta), buf(ta)],
        compiler_params=_params(3), name="dilated_attention",
    )(proj["q"], proj["k"], proj["v"], jnp.tile(q_norm, 2).reshape(1, LANES), jnp.tile(k_norm, 2).reshape(1, LANES),
      jnp.asarray(bsum, bf16))


def _memkv_kernel(mem_ref, gm_ref, wkv_ref, gk_ref, k_ref, v_ref):
    hm = _rms(mem_ref[...], gm_ref[...]).astype(bf16)
    kv = _dot(hm, wkv_ref[...])
    gk = gk_ref[...]
    for hh in range(CA_HEADS):
        sl = slice(hh * CA_HEAD_DIM, (hh + 1) * CA_HEAD_DIM)
        k_ref[:, sl] = _rms(kv[:, sl], gk).astype(bf16)
    v_ref[...] = kv[:, D_MODEL:].astype(bf16)


def _memory_kv(mem, norm_mem, wkv_b, k_norm):
    bsz, m, _ = mem.shape
    const = lambda b: (0, 0)
    blk = pl.BlockSpec((None, m, D_MODEL), lambda b: (b, 0, 0))
    return pl.pallas_call(
        _memkv_kernel, grid=(bsz,),
        in_specs=[blk, pl.BlockSpec((1, D_MODEL), const), pl.BlockSpec((D_MODEL, 2 * D_MODEL), const),
                  pl.BlockSpec((1, CA_HEAD_DIM), const)],
        out_specs=[blk, blk],
        out_shape=[jax.ShapeDtypeStruct((bsz, m, D_MODEL), bf16)] * 2,
        compiler_params=_params(1), name="memory_kv",
    )(mem, norm_mem.reshape(1, -1), wkv_b, k_norm.reshape(1, -1))


def _outca_kernel(x_ref, y0_ref, y1_ref, y2_ref, y3_ref, w0_ref, w1_ref, w2_ref, w3_ref, gca_ref, wq_ref, kn_ref,
                  v_ref, gq_ref, wo_ref, o_ref):
    x1 = x_ref[...]
    for y_ref, w_ref in ((y0_ref, w0_ref), (y1_ref, w1_ref), (y2_ref, w2_ref), (y3_ref, w3_ref)):
        x1 = x1 + _dot(y_ref[...].astype(bf16), w_ref[...])
    h = _rms(x1, gca_ref[...]).astype(bf16)
    q = _dot(h, wq_ref[...])
    gq = gq_ref[...]
    outs = []
    for hh in range(CA_HEADS):
        sl = slice(hh * CA_HEAD_DIM, (hh + 1) * CA_HEAD_DIM)
        qn = (_rms(q[:, sl], gq) * (CA_HEAD_DIM ** -0.5)).astype(bf16)
        s = _dot_nt(qn, kn_ref[:, sl])
        m = jnp.max(s, axis=-1, keepdims=True)
        p = jnp.exp(s - m)
        l = jnp.sum(p, axis=-1, keepdims=True)
        outs.append((_dot(p.astype(bf16), v_ref[:, sl]) / l).astype(bf16))
    o_ref[...] = x1 + _dot(jnp.concatenate(outs, axis=-1), wo_ref[...])


def _outproj_cross_attention(x2d, ys, w_out_b, bsz, seq, norm_ca, wq_b, kn, vv, q_norm, wo_b):
    t = bsz * seq
    tm = TM_CA
    ns = seq // tm
    m = kn.shape[1]
    row = lambda b, i: (b * ns + i, 0)
    const = lambda b, i: (0, 0)
    mix = pl.BlockSpec((tm, GROUP_WIDTH), row)
    wblk = pl.BlockSpec((GROUP_WIDTH, D_MODEL), const)
    sq = pl.BlockSpec((D_MODEL, D_MODEL), const)
    kvb = pl.BlockSpec((None, m, D_MODEL), lambda b, i: (b, 0, 0))
    w_parts = [w_out_b[j * GROUP_WIDTH:(j + 1) * GROUP_WIDTH] for j in range(4)]
    return pl.pallas_call(
        _outca_kernel, grid=(bsz, ns),
        in_specs=[pl.BlockSpec((tm, D_MODEL), row), mix, mix, mix, mix, wblk, wblk, wblk, wblk,
                  pl.BlockSpec((1, D_MODEL), const), sq, kvb, kvb, pl.BlockSpec((1, CA_HEAD_DIM), const), sq],
        out_specs=pl.BlockSpec((tm, D_MODEL), row),
        out_shape=jax.ShapeDtypeStruct((t, D_MODEL), f32),
        compiler_params=_params(2), name="outproj_cross_attention",
    )(x2d, *ys, *w_parts, norm_ca.reshape(1, -1), wq_b, kn, vv, q_norm.reshape(1, -1), wo_b)


def _to_token_tiles(ref, x):
    rows = x.shape[0]
    for s in range(TOKEN_TILE_ROWS):
        ref[pl.ds(s, rows, stride=TOKEN_TILE_ROWS), :] = x[:, s * LANES:(s + 1) * LANES]


def _from_token_tiles(ref, rows):
    return jnp.concatenate([ref[pl.ds(s, rows, stride=TOKEN_TILE_ROWS), :] for s in range(TOKEN_TILE_ROWS)], axis=-1)


def _router_kernel(x_ref, g_ref, w1_ref, w2_ref, b_ref, slt_ref, h_ref, meta_ref, g1_ref, g2_ref, cnt_ref, base):
    @pl.when(pl.program_id(0) == 0)
    def _():
        base[...] = jnp.zeros_like(base)

    tm = TM_ROUTER
    h = _rms(x_ref[...], g_ref[...])
    _to_token_tiles(h_ref, h)
    h1, h2 = _split2(h)
    w1 = w1_ref[...]
    logits = (_dot(h1, w1) + _dot(h1, w2_ref[...]) + _dot(h2, w1)) + b_ref[...]
    lane = lax.broadcasted_iota(jnp.int32, (tm, LANES), 1)
    lanef = lane.astype(f32)
    neg = -jnp.inf

    lgm = jnp.where((lane >= N_EXPERTS) & (lane < N_EXPERTS + N_EXPERT_GROUPS), logits, neg)
    mg = jnp.max(lgm, axis=-1, keepdims=True)
    g_p = 1.0 / jnp.sum(jnp.exp(lgm - mg), axis=-1, keepdims=True)
    gi = jnp.min(jnp.where(lgm == mg, lanef, float(LANES)), axis=-1, keepdims=True) - float(N_EXPERTS)

    grp = (lane >> (EXPERTS_PER_GROUP.bit_length() - 1)).astype(f32)
    lem = jnp.where((grp == gi) & (lane < N_EXPERTS), logits, neg)
    me = jnp.max(lem, axis=-1, keepdims=True)
    i1 = jnp.min(jnp.where(lem == me, lanef, float(LANES)), axis=-1, keepdims=True)
    le2 = jnp.where(lanef == i1, neg, lem)
    m2 = jnp.max(le2, axis=-1, keepdims=True)
    i2 = jnp.min(jnp.where(le2 == m2, lanef, float(LANES)), axis=-1, keepdims=True)
    e2 = jnp.exp(m2 - me)
    gate1 = g_p / (1.0 + e2)
    gate2 = g_p * e2 / (1.0 + e2)

    oh1 = (lanef == i1).astype(f32)
    oh2 = (lanef == i2).astype(f32)
    cnt = oh1 + oh2
    before = _dot(slt_ref[...], cnt.astype(bf16)) + base[...]
    r1 = jnp.sum(oh1 * before, axis=-1, keepdims=True)
    r2 = jnp.sum(oh2 * before, axis=-1, keepdims=True)
    total = base[...] + jnp.sum(cnt, axis=0, keepdims=True)
    base[...] = total
    cnt_ref[...] = total

    packed = jnp.where(lane == 0, i1, jnp.where(lane == 1, i2, jnp.where(lane == 2, r1, r2)))
    meta_ref[...] = packed.T[0:SUBLANES, :].astype(jnp.int32)
    g1_ref[...] = jnp.broadcast_to(gate1, (tm, LANES))
    g2_ref[...] = jnp.broadcast_to(gate2, (tm, LANES))


def _router(x2d, norm_g, wg, bg, we, be):
    t = x2d.shape[0]
    tm = TM_ROUTER
    pad = LANES - N_EXPERTS - N_EXPERT_GROUPS
    wr = jnp.pad(jnp.concatenate([we, wg], axis=1), ((0, 0), (0, pad)))
    br = jnp.pad(jnp.concatenate([be, bg]), (0, pad)).reshape(1, LANES)
    w1, w2 = _split2(wr)
    slt = jnp.asarray(np.tril(np.ones((tm, tm), np.float32), -1), bf16)
    const = lambda i: (0, 0)
    row = lambda i: (i, 0)
    wspec = pl.BlockSpec((D_MODEL, LANES), const)
    gate = pl.BlockSpec((tm, LANES), row)
    return pl.pallas_call(
        _router_kernel, grid=(t // tm,),
        in_specs=[pl.BlockSpec((tm, D_MODEL), row), pl.BlockSpec((1, D_MODEL), const), wspec, wspec,
                  pl.BlockSpec((1, LANES), const), pl.BlockSpec((tm, tm), const)],
        out_specs=[pl.BlockSpec((tm * TOKEN_TILE_ROWS, LANES), row), pl.BlockSpec((SUBLANES, tm), lambda i: (0, i)),
                   gate, gate, pl.BlockSpec((1, LANES), const)],
        out_shape=[jax.ShapeDtypeStruct((t * TOKEN_TILE_ROWS, LANES), f32), jax.ShapeDtypeStruct((SUBLANES, t), jnp.int32),
                   jax.ShapeDtypeStruct((t, LANES), f32), jax.ShapeDtypeStruct((t, LANES), f32),
                   jax.ShapeDtypeStruct((1, LANES), f32)],
        scratch_shapes=[pltpu.VMEM((1, LANES), f32)],
        compiler_params=_params(1), name="moe_router",
    )(x2d, norm_g.reshape(1, -1), w1, w2, br, slt)


def _tile_copy(src, src_tile, dst, dst_tile, sem):
    rows = TOKEN_TILE_ROWS
    return pltpu.make_async_copy(src.at[pl.ds(pl.multiple_of(src_tile * rows, rows), rows)],
                                 dst.at[pl.ds(pl.multiple_of(dst_tile * rows, rows), rows)], sem)


def _dispatch_kernel(starts_ref, e1_ref, e2_ref, r1_ref, r2_ref, h_ref, init_ref, buf_ref, sem):
    del init_ref

    def issue(j, carry):
        _tile_copy(h_ref, j, buf_ref, starts_ref[e1_ref[j]] + r1_ref[j], sem).start(priority=0)
        _tile_copy(h_ref, j, buf_ref, starts_ref[e2_ref[j]] + r2_ref[j], sem).start(priority=1)
        return carry

    lax.fori_loop(0, TD_DISPATCH, issue, 0, unroll=DMA_UNROLL)

    def drain(j, carry):
        _tile_copy(h_ref, 0, buf_ref, 0, sem).wait()
        _tile_copy(h_ref, 0, buf_ref, 0, sem).wait()
        return carry

    lax.fori_loop(0, TD_DISPATCH, drain, 0, unroll=DMA_UNROLL)


def _dispatch(h_tiles, meta, starts_pad, init_buf):
    t = meta.shape[1]
    td = TD_DISPATCH
    smem = pl.BlockSpec((td,), lambda i, s: (i,), memory_space=pltpu.SMEM)
    anyspec = pl.BlockSpec(memory_space=pl.ANY)
    hspec = pl.BlockSpec((td * TOKEN_TILE_ROWS, LANES), lambda i, s: (i, 0))
    grid_spec = pltpu.PrefetchScalarGridSpec(
        num_scalar_prefetch=1, grid=(t // td,), in_specs=[smem, smem, smem, smem, hspec, anyspec], out_specs=anyspec,
        scratch_shapes=[pltpu.SemaphoreType.DMA(())])
    return pl.pallas_call(
        _dispatch_kernel, grid_spec=grid_spec, out_shape=jax.ShapeDtypeStruct(init_buf.shape, f32),
        input_output_aliases={6: 0}, compiler_params=_params(1), name="moe_dispatch",
    )(starts_pad, meta[0], meta[1], meta[2], meta[3], h_tiles, init_buf)


def _expert_kernel(blk_e_ref, nused_ref, x_ref, wg_ref, wu_ref, wd_ref, o_ref):
    del blk_e_ref
    used = pl.program_id(0) < nused_ref[0]

    @pl.when(used)
    def _():
        x = _from_token_tiles(x_ref, MOE_BLOCK).astype(bf16)
        g = _dot(x, wg_ref[...].astype(bf16))
        u = _dot(x, wu_ref[...].astype(bf16))
        _to_token_tiles(o_ref, _dot((_silu(g) * u).astype(bf16), wd_ref[...].astype(bf16)))

    @pl.when(jnp.logical_not(used))
    def _():
        o_ref[...] = jnp.zeros_like(o_ref)


def _experts(buf, blk_e, n_used, layer, w_gate, w_up, w_down):
    n_blocks = buf.shape[0] // (MOE_BLOCK * TOKEN_TILE_ROWS)
    xspec = pl.BlockSpec((MOE_BLOCK * TOKEN_TILE_ROWS, LANES), lambda i, be, nu: (i, 0))
    wspec = lambda a, b: pl.BlockSpec((None, None, a, b), lambda i, be, nu: (layer, be[i], 0, 0))
    grid_spec = pltpu.PrefetchScalarGridSpec(
        num_scalar_prefetch=2, grid=(n_blocks,),
        in_specs=[xspec, wspec(D_MODEL, EXPERT_HIDDEN), wspec(D_MODEL, EXPERT_HIDDEN), wspec(EXPERT_HIDDEN, D_MODEL)],
        out_specs=xspec)
    return pl.pallas_call(
        _expert_kernel, grid_spec=grid_spec, out_shape=jax.ShapeDtypeStruct(buf.shape, f32),
        compiler_params=_params(1), name="moe_experts",
    )(blk_e, n_used, buf, w_gate, w_up, w_down)


def _combine_kernel(starts_ref, e1_ref, e2_ref, r1_ref, r2_ref, x_ref, g1_ref, g2_ref, yb_ref, o_ref, rows0, rows1, sem):
    tc = TC_COMBINE

    def issue(j, carry):
        _tile_copy(yb_ref, starts_ref[e1_ref[j]] + r1_ref[j], rows0, j, sem).start(priority=0)
        _tile_copy(yb_ref, starts_ref[e2_ref[j]] + r2_ref[j], rows1, j, sem).start(priority=1)
        return carry

    lax.fori_loop(0, tc, issue, 0, unroll=DMA_UNROLL)

    def drain(j, carry):
        _tile_copy(yb_ref, 0, rows0, 0, sem).wait()
        _tile_copy(yb_ref, 0, rows1, 0, sem).wait()
        return carry

    lax.fori_loop(0, tc, drain, 0, unroll=DMA_UNROLL)
    reps = D_MODEL // LANES
    g1 = jnp.concatenate([g1_ref[...]] * reps, axis=-1)
    g2 = jnp.concatenate([g2_ref[...]] * reps, axis=-1)
    o_ref[...] = x_ref[...] + (g1 * _from_token_tiles(rows0, tc) + g2 * _from_token_tiles(rows1, tc))


def _combine(x2d, g1, g2, yb, meta, starts_pad):
    t = x2d.shape[0]
    tc = TC_COMBINE
    smem = pl.BlockSpec((tc,), lambda i, s: (i,), memory_space=pltpu.SMEM)
    row = lambda i, s: (i, 0)
    tiles = pltpu.VMEM((tc * TOKEN_TILE_ROWS, LANES), f32)
    grid_spec = pltpu.PrefetchScalarGridSpec(
        num_scalar_prefetch=1, grid=(t // tc,),
        in_specs=[smem, smem, smem, smem, pl.BlockSpec((tc, D_MODEL), row), pl.BlockSpec((tc, LANES), row),
                  pl.BlockSpec((tc, LANES), row), pl.BlockSpec(memory_space=pl.ANY)],
        out_specs=pl.BlockSpec((tc, D_MODEL), row),
        scratch_shapes=[tiles, tiles, pltpu.SemaphoreType.DMA(())])
    return pl.pallas_call(
        _combine_kernel, grid_spec=grid_spec, out_shape=jax.ShapeDtypeStruct((t, D_MODEL), f32),
        compiler_params=_params(1), name="moe_combine",
    )(starts_pad, meta[0], meta[1], meta[2], meta[3], x2d, g1, g2, yb)


def _moe(x2d, norm_g, wg, bg, we, be, layer, w_gate, w_up, w_down, init_buf):
    t = x2d.shape[0]
    h_tiles, meta, g1, g2, cnt = _router(x2d, norm_g, wg, bg, we, be)
    counts = cnt[0, :N_EXPERTS].astype(jnp.int32)
    padded = ((counts + MOE_BLOCK - 1) // MOE_BLOCK) * MOE_BLOCK
    ends_pad = jnp.cumsum(padded)
    starts_pad = (ends_pad - padded).astype(jnp.int32)
    a = t * TOP_K
    p_rows = (-(-a // MOE_BLOCK)) * MOE_BLOCK + N_EXPERTS * MOE_BLOCK
    n_blocks = p_rows // MOE_BLOCK
    blk_start = jnp.arange(n_blocks, dtype=jnp.int32) * MOE_BLOCK
    blk_e = jnp.minimum(jnp.sum((ends_pad[None, :] <= blk_start[:, None]).astype(jnp.int32), axis=1), N_EXPERTS - 1)
    n_used = (ends_pad[-1:] // MOE_BLOCK).astype(jnp.int32)
    if init_buf is None:
        init_buf = jnp.zeros((p_rows * TOKEN_TILE_ROWS, LANES), f32)
    buf = _dispatch(h_tiles, meta, starts_pad, init_buf)
    yb = _experts(buf, blk_e, n_used, layer, w_gate, w_up, w_down)
    return _combine(x2d, g1, g2, yb, meta, starts_pad), buf


def _prep_w_in_kernel(w_ref, o_ref):
    cut = GROUP_WIDTH + SSD_CONV_CH
    o_ref[:, 0:cut] = w_ref[:, 0:cut].astype(bf16)
    lane = lax.broadcasted_iota(jnp.int32, (W_PREP_ROWS, LANES), 1)
    o_ref[:, cut:cut + LANES] = jnp.where(lane < SSD_HEADS, w_ref[:, cut:cut + LANES], 0.0).astype(bf16)
    tail = w_ref[:, cut:N_IN_RAW]
    o_ref[:, cut + LANES:N_PROJ] = tail[:, SSD_HEADS:SSD_HEADS + N_PROJ - cut - LANES].astype(bf16)


def _prep_w_in(w_in):
    depth = w_in.shape[0]
    rows = W_PREP_ROWS
    return pl.pallas_call(
        _prep_w_in_kernel, grid=(depth, D_MODEL // rows),
        in_specs=[pl.BlockSpec((None, rows, N_IN_RAW), lambda l, i: (l, i, 0))],
        out_specs=pl.BlockSpec((None, rows, N_PROJ), lambda l, i: (l, i, 0)),
        out_shape=jax.ShapeDtypeStruct((depth, D_MODEL, N_PROJ), bf16),
        compiler_params=_params(2), name="prep_w_in",
    )(w_in)


def kernel(x, mem, norm_mix, w_in, ssd_conv_w, ssd_conv_b, ssd_dt_bias, ssd_a_log, ssd_d, ssd_norm, sc_conv_w,
           attn_q_norm, attn_k_norm, cf_conv_w, cf_conv_b, cf_ln_g, cf_ln_b, w_out, norm_ca, norm_mem, ca_wq, ca_wkv,
           ca_q_norm, ca_k_norm, ca_wo, norm_ffn, router_group_w, router_group_b, router_expert_w, router_expert_b,
           exp_w_gate, exp_w_up, exp_w_down):
    bsz, seq, d = x.shape
    depth = w_in.shape[0]
    assert d == D_MODEL and seq % ATT_TILE == 0 and seq % TS_SSD == 0 and (bsz * seq) % TM_PROJ == 0
    x2d = x.reshape(bsz * seq, d)
    w_cat = _prep_w_in(w_in)
    moe_buf = None
    for l in range(depth):
        proj = _in_projection(x2d, norm_mix[l].reshape(1, -1), l, w_cat)
        y_ssd = _ssd(proj, bsz, seq, ssd_conv_w[l], ssd_conv_b[l], ssd_dt_bias[l], ssd_a_log[l], ssd_d[l], ssd_norm[l])
        y_sc, y_cf = _convs(proj, bsz, seq, sc_conv_w[l], cf_conv_w[l], cf_conv_b[l], cf_ln_g[l], cf_ln_b[l])
        y_att = _dilated_attention(proj, bsz, seq, attn_q_norm[l], attn_k_norm[l])
        kn, vv = _memory_kv(mem, norm_mem[l], ca_wkv[l].astype(bf16), ca_k_norm[l])
        x2d = _outproj_cross_attention(x2d, (y_ssd, y_sc, y_att, y_cf), w_out[l].astype(bf16), bsz, seq, norm_ca[l],
                                       ca_wq[l].astype(bf16), kn, vv, ca_q_norm[l], ca_wo[l].astype(bf16))
        x2d, moe_buf = _moe(x2d, norm_ffn[l], router_group_w[l], router_group_b[l], router_expert_w[l],
                            router_expert_b[l], l, exp_w_gate, exp_w_up, exp_w_down, moe_buf)
    return x2d.reshape(bsz, seq, d)
```

```python
import functools

import numpy as np
import jax
import jax.numpy as jnp
from jax import lax
from jax.experimental import pallas as pl
from jax.experimental.pallas import tpu as pltpu

f32 = jnp.float32
bf16 = jnp.bfloat16

D_MODEL = 1024
GROUP_WIDTH = 384
SSD_HEADS = 6
SSD_HEAD_DIM = 64
SSD_GROUPS = 2
SSD_STATE = 128
SSD_CONV = 4
SSD_CHUNK = 128
SSD_CONV_CH = GROUP_WIDTH + 2 * SSD_GROUPS * SSD_STATE
SC_WIDTH = 3
ATT_HEADS = 6
ATT_HEAD_DIM = 64
ATT_BLOCK = 128
DILATIONS = (1, 4, 16)
ATT_TILE = ATT_BLOCK * max(DILATIONS)
ATT_GROUPS = {1: 8, 4: 8, 16: 8}
CF_KERNEL = 31
CA_HEADS = 4
CA_HEAD_DIM = D_MODEL // CA_HEADS
N_EXPERT_GROUPS = 4
EXPERTS_PER_GROUP = 8
N_EXPERTS = N_EXPERT_GROUPS * EXPERTS_PER_GROUP
TOP_K = 2
EXPERT_HIDDEN = D_MODEL // 2
MOE_BLOCK = 512
RMS_EPS = 1e-6
LOG2_E = 1.4426950408889634
LN_EPS = 1e-5

LANES = 128
SUBLANES = 8
VMEM_LIMIT_BYTES = 56 * 1024 * 1024

TM_PROJ = 512
TS_SSD = 1024
SSD_UNROLL = 8
TS_CONV = 512
CONV_ROWS = 64
TM_CA = 512
TM_ROUTER = 512
TD_DISPATCH = 1024
TC_COMBINE = 1024
TOKEN_TILE_ROWS = 8
DMA_UNROLL = 8

_SEGS = (("z", GROUP_WIDTH), ("xbc", SSD_CONV_CH), ("dt", LANES), ("sc_b", GROUP_WIDTH), ("sc_c", GROUP_WIDTH),
         ("sc_h", GROUP_WIDTH), ("q", GROUP_WIDTH), ("k", GROUP_WIDTH), ("v", GROUP_WIDTH), ("cf_a", GROUP_WIDTH),
         ("cf_g", GROUP_WIDTH))
N_PROJ = sum(w for _, w in _SEGS)
N_IN_RAW = N_PROJ - LANES + SSD_HEADS
W_PREP_ROWS = 256


def _params(n_axes):
    return pltpu.CompilerParams(dimension_semantics=("arbitrary",) * n_axes, vmem_limit_bytes=VMEM_LIMIT_BYTES)


def _dot(a, b):
    return jnp.dot(a, b, preferred_element_type=f32)


def _dot_nt(a, b):
    return lax.dot_general(a, b, (((1,), (1,)), ((), ())), preferred_element_type=f32)


def _split2(x):
    hi = x.astype(bf16)
    lo = (x - hi.astype(f32)).astype(bf16)
    return hi, lo


def _split3(x):
    hi = x.astype(bf16)
    r = x - hi.astype(f32)
    mid = r.astype(bf16)
    lo = (r - mid.astype(f32)).astype(bf16)
    return hi, mid, lo


def _dot_exact_rhs(x, m_bf16, n_split):
    parts = _split3(x) if n_split == 3 else _split2(x)
    acc = _dot(parts[0], m_bf16)
    for p in parts[1:]:
        acc = acc + _dot(p, m_bf16)
    return acc


def _sigmoid(x):
    return 1.0 / (1.0 + jnp.exp(-x))


def _silu(x):
    return x * _sigmoid(x)


def _rms(x, gain):
    ms = jnp.mean(x * x, axis=-1, keepdims=True)
    return x * lax.rsqrt(ms + RMS_EPS) * gain


def _inproj_kernel(x_ref, g_ref, w_ref, *o_refs):
    h = _rms(x_ref[...], g_ref[...]).astype(bf16)
    off = 0
    for (_, w), o_ref in zip(_SEGS, o_refs):
        o_ref[...] = _dot(h, w_ref[:, off:off + w])
        off += w


def _in_projection(x2d, gain, layer, w_cat):
    t = x2d.shape[0]
    tm = TM_PROJ
    in_specs = [pl.BlockSpec((tm, D_MODEL), lambda i: (i, 0)), pl.BlockSpec((1, D_MODEL), lambda i: (0, 0)),
                pl.BlockSpec((None, D_MODEL, N_PROJ), lambda i: (layer, 0, 0))]
    out_specs = [pl.BlockSpec((tm, w), lambda i: (i, 0)) for _, w in _SEGS]
    out_shape = [jax.ShapeDtypeStruct((t, w), f32) for _, w in _SEGS]
    outs = pl.pallas_call(
        _inproj_kernel, grid=(t // tm,), in_specs=in_specs, out_specs=out_specs, out_shape=out_shape,
        compiler_params=_params(1), name="in_projection",
    )(x2d, gain, w_cat)
    return dict(zip([n for n, _ in _SEGS], outs))


def _ssd_kernel(xbc_ref, dt_ref, z_ref, cw_ref, cb_ref, dtb_ref, alog_ref, dvec_ref, ng_ref, tril_ref, e384_ref,
                e768_ref, o_ref, xbuf, xc, state):
    ts = TS_SSD
    q = SSD_CHUNK
    halo = SUBLANES

    @pl.when(pl.program_id(1) == 0)
    def _():
        xbuf[0:halo, :] = jnp.zeros((halo, SSD_CONV_CH), f32)
        state[...] = jnp.zeros_like(state)

    xbuf[halo:halo + ts, :] = xbc_ref[...]
    acc = cb_ref[...] + cw_ref[0:1, :] * xbuf[halo - 3:halo - 3 + ts, :]
    for k in range(1, SSD_CONV):
        acc = acc + cw_ref[k:k + 1, :] * xbuf[halo - 3 + k:halo - 3 + k + ts, :]
    xc[...] = _silu(acc)
    xbuf[0:halo, :] = xbuf[ts:ts + halo, :]

    lane_row = lax.broadcasted_iota(jnp.int32, (1, LANES), 1)
    head_lane = lane_row < SSD_HEADS
    a_row = jnp.where(head_lane, -jnp.exp(alog_ref[...]), 0.0)
    lane_sq = lax.broadcasted_iota(jnp.int32, (q, LANES), 1)
    row_sq = lax.broadcasted_iota(jnp.int32, (q, LANES), 0)
    causal = row_sq >= lane_sq
    left = lane_sq < SSD_HEAD_DIM
    tril = tril_ref[...]
    e384 = e384_ref[...]
    e768 = e768_ref[...]
    dvec = dvec_ref[...]
    ng = ng_ref[...]

    def chunk(c, carry):
        r0 = pl.multiple_of(c * q, q)
        dtr = dt_ref[pl.ds(r0, q), :] + dtb_ref[...]
        sp = jnp.maximum(dtr, 0.0) + jnp.log1p(jnp.exp(-jnp.abs(dtr)))
        dt = jnp.where(head_lane, sp, 0.0)
        da = dt * a_row
        d3 = _split3(da)
        cs = _dot(tril, d3[0]) + _dot(tril, d3[1]) + _dot(tril, d3[2])
        cs_t = cs.T
        c3 = _split3(cs)
        cs384 = _dot(c3[0], e384) + _dot(c3[1], e384) + _dot(c3[2], e384)
        cs768 = _dot(c3[0], e768) + _dot(c3[1], e768) + _dot(c3[2], e768)
        dt384 = _dot_exact_rhs(dt, e384, 2)
        last384 = cs384[q - 1:q, :]
        decay384 = jnp.exp(last384 - cs384)
        expcs384 = jnp.exp(cs384)

        xs = xc[pl.ds(r0, q), 0:GROUP_WIDTH]
        bm = xc[pl.ds(r0, q), GROUP_WIDTH:GROUP_WIDTH + SSD_GROUPS * SSD_STATE]
        cm = xc[pl.ds(r0, q), GROUP_WIDTH + SSD_GROUPS * SSD_STATE:SSD_CONV_CH]
        xdt = xs * dt384
        xdt_b = xdt.astype(bf16)
        xdec_b = (xdt * decay384).astype(bf16)
        st = state[...]
        st_b = st.astype(bf16)

        cb_g, bt_g, c_g = [], [], []
        for g in range(SSD_GROUPS):
            b_f = bm[:, g * SSD_STATE:(g + 1) * SSD_STATE]
            c_b = cm[:, g * SSD_STATE:(g + 1) * SSD_STATE].astype(bf16)
            cb_g.append(_dot_nt(c_b, b_f.astype(bf16)))
            bt_g.append(b_f.T.astype(bf16))
            c_g.append(c_b)

        y_parts, st_parts = [], []
        for p in range(SSD_HEADS // 2):
            sl = slice(p * LANES, (p + 1) * LANES)
            xdt_p, xdec_p, st_p = xdt_b[:, sl], xdec_b[:, sl], st_b[:, sl]
            yd, s_new, yo = [], {}, {}
            for hh in range(2):
                h = 2 * p + hh
                g = h // (SSD_HEADS // SSD_GROUPS)
                diff = cs768[:, h * LANES:(h + 1) * LANES] - cs_t[h:h + 1, :]
                seg = jnp.exp(jnp.where(causal, diff, -1e30))
                m = (cb_g[g] * seg).astype(bf16)
                yd.append(_dot(m, xdt_p))
                if g not in s_new:
                    s_new[g] = _dot(bt_g[g], xdec_p)
                    yo[g] = _dot(c_g[g], st_p)
            g0 = (2 * p) // (SSD_HEADS // SSD_GROUPS)
            g1 = (2 * p + 1) // (SSD_HEADS // SSD_GROUPS)
            y_diag = jnp.where(left, yd[0], yd[1])
            y_off = jnp.where(left, yo[g0], yo[g1])
            s_pair = jnp.where(left, s_new[g0], s_new[g1])
            y_parts.append(y_diag + y_off * expcs384[:, sl] + xs[:, sl] * dvec[:, sl])
            st_parts.append(st[:, sl] * expcs384[q - 1:q, sl] + s_pair)
        state[...] = jnp.concatenate(st_parts, axis=-1)
        y = jnp.concatenate(y_parts, axis=-1)
        yg = y * _silu(z_ref[pl.ds(r0, q), :])
        o_ref[pl.ds(r0, q), :] = _rms(yg, ng)
        return carry

    lax.fori_loop(0, ts // q, chunk, 0, unroll=SSD_UNROLL)


def _ssd_constants():
    q = SSD_CHUNK
    tril = np.tril(np.ones((q, q), np.float32))
    e384 = np.zeros((LANES, GROUP_WIDTH), np.float32)
    e768 = np.zeros((LANES, SSD_HEADS * LANES), np.float32)
    for h in range(SSD_HEADS):
        e384[h, h * SSD_HEAD_DIM:(h + 1) * SSD_HEAD_DIM] = 1.0
        e768[h, h * LANES:(h + 1) * LANES] = 1.0
    return jnp.asarray(tril, bf16), jnp.asarray(e384, bf16), jnp.asarray(e768, bf16)


def _ssd(proj, bsz, seq, conv_w, conv_b, dt_bias, a_log, d_skip, norm_g):
    t = bsz * seq
    ts = TS_SSD
    ns = seq // ts
    tril, e384, e768 = _ssd_constants()
    pad = LANES - SSD_HEADS
    dtb = jnp.pad(dt_bias, (0, pad)).reshape(1, LANES)
    alog = jnp.pad(a_log, (0, pad)).reshape(1, LANES)
    dvec = jnp.repeat(d_skip, SSD_HEAD_DIM).reshape(1, GROUP_WIDTH)
    row = lambda b, i: (b * ns + i, 0)
    const = lambda b, i: (0, 0)
    return pl.pallas_call(
        _ssd_kernel, grid=(bsz, ns),
        in_specs=[pl.BlockSpec((ts, SSD_CONV_CH), row), pl.BlockSpec((ts, LANES), row), pl.BlockSpec((ts, GROUP_WIDTH), row),
                  pl.BlockSpec((SSD_CONV, SSD_CONV_CH), const), pl.BlockSpec((1, SSD_CONV_CH), const),
                  pl.BlockSpec((1, LANES), const), pl.BlockSpec((1, LANES), const), pl.BlockSpec((1, GROUP_WIDTH), const),
                  pl.BlockSpec((1, GROUP_WIDTH), const), pl.BlockSpec((SSD_CHUNK, SSD_CHUNK), const),
                  pl.BlockSpec((LANES, GROUP_WIDTH), const), pl.BlockSpec((LANES, SSD_HEADS * LANES), const)],
        out_specs=pl.BlockSpec((ts, GROUP_WIDTH), row),
        out_shape=jax.ShapeDtypeStruct((t, GROUP_WIDTH), f32),
        scratch_shapes=[pltpu.VMEM((ts + SUBLANES, SSD_CONV_CH), f32), pltpu.VMEM((ts, SSD_CONV_CH), f32),
                        pltpu.VMEM((SSD_STATE, GROUP_WIDTH), f32)],
        compiler_params=_params(2), name="ssd_scan",
    )(proj["xbc"], proj["dt"], proj["z"], conv_w, conv_b.reshape(1, -1), dtb, alog, dvec, norm_g.reshape(1, -1), tril,
      e384, e768)


_CF_HALO = 32


def _conv_kernel(scb_ref, scc_ref, sch_ref, cfa_ref, cfg_ref, scw_ref, cfw_ref, cfb_ref, lng_ref, lnb_ref, ysc_ref,
                 ycf_ref, ubuf, gbuf):
    ts = TS_CONV
    uh = SUBLANES

    @pl.when(pl.program_id(1) == 0)
    def _():
        ubuf[0:uh, :] = jnp.zeros((uh, GROUP_WIDTH), f32)
        gbuf[0:_CF_HALO, :] = jnp.zeros((_CF_HALO, GROUP_WIDTH), f32)

    ubuf[uh:uh + ts, :] = scc_ref[...] * sch_ref[...]
    gbuf[_CF_HALO:_CF_HALO + ts, :] = cfa_ref[...] * _sigmoid(cfg_ref[...])

    o = uh - (SC_WIDTH - 1)
    acc = scw_ref[0:1, :] * ubuf[o:o + ts, :]
    for k in range(1, SC_WIDTH):
        acc = acc + scw_ref[k:k + 1, :] * ubuf[o + k:o + k + ts, :]
    ysc_ref[...] = scb_ref[...] * acc

    o = _CF_HALO - (CF_KERNEL - 1)
    cfb = cfb_ref[...]
    lng = lng_ref[...]
    lnb = lnb_ref[...]
    for c in range(ts // CONV_ROWS):
        r0 = c * CONV_ROWS
        acc = cfb
        for s in range(SUBLANES):
            rows = CONV_ROWS + (SUBLANES if s else 0)
            part = None
            for k in range(CF_KERNEL):
                if (o + k) % SUBLANES != s:
                    continue
                a = r0 + o + k - s
                term = cfw_ref[k:k + 1, :] * gbuf[a:a + rows, :]
                part = term if part is None else part + term
            acc = acc + part[s:s + CONV_ROWS, :]
        mu = jnp.mean(acc, axis=-1, keepdims=True)
        xc = acc - mu
        var = jnp.mean(xc * xc, axis=-1, keepdims=True)
        y = xc * lax.rsqrt(var + LN_EPS) * lng + lnb
        ycf_ref[r0:r0 + CONV_ROWS, :] = _silu(y)

    ubuf[0:uh, :] = ubuf[ts:ts + uh, :]
    gbuf[0:_CF_HALO, :] = gbuf[ts:ts + _CF_HALO, :]


def _convs(proj, bsz, seq, sc_w, cf_w, cf_b, ln_g, ln_b):
    t = bsz * seq
    ts = TS_CONV
    ns = seq // ts
    row = lambda b, i: (b * ns + i, 0)
    const = lambda b, i: (0, 0)
    act = pl.BlockSpec((ts, GROUP_WIDTH), row)
    vec = pl.BlockSpec((1, GROUP_WIDTH), const)
    return pl.pallas_call(
        _conv_kernel, grid=(bsz, ns),
        in_specs=[act, act, act, act, act, pl.BlockSpec((SC_WIDTH, GROUP_WIDTH), const),
                  pl.BlockSpec((CF_KERNEL, GROUP_WIDTH), const), vec, vec, vec],
        out_specs=[act, act],
        out_shape=[jax.ShapeDtypeStruct((t, GROUP_WIDTH), f32)] * 2,
        scratch_shapes=[pltpu.VMEM((ts + SUBLANES, GROUP_WIDTH), f32), pltpu.VMEM((ts + _CF_HALO, GROUP_WIDTH), f32)],
        compiler_params=_params(2), name="gated_convs",
    )(proj["sc_b"], proj["sc_c"], proj["sc_h"], proj["cf_a"], proj["cf_g"], sc_w, cf_w, cf_b.reshape(1, -1),
      ln_g.reshape(1, -1), ln_b.reshape(1, -1))


def _attn_kernel(q_ref, k_ref, v_ref, qg_ref, kg_ref, bsum_ref, o_ref, qbuf, kbuf, vbuf, acc_o, acc_m, acc_l):
    ta = ATT_TILE
    n = ATT_BLOCK
    dmax = max(DILATIONS)
    tile = pl.program_id(2)

    @pl.when(tile == 0)
    def _():
        kbuf[0:ta, :] = jnp.zeros((ta, LANES), f32)
        vbuf[0:ta, :] = jnp.zeros((ta, LANES), f32)

    bsum = bsum_ref[...]

    def head_rms(x, g):
        ms = _dot_exact_rhs(x * x, bsum, 2)
        return x * lax.rsqrt(ms + RMS_EPS) * g

    qg = qg_ref[...] * (ATT_HEAD_DIM ** -0.5 * LOG2_E)
    kg = kg_ref[...]
    for r in range(dmax):
        res = pl.ds(r, n, stride=dmax)
        qbuf[r * n:(r + 1) * n, :] = head_rms(q_ref[res, :], qg)
        kbuf[ta + r * n:ta + (r + 1) * n, :] = head_rms(k_ref[res, :], kg)
        vbuf[ta + r * n:ta + (r + 1) * n, :] = v_ref[res, :]

    qa = lax.broadcasted_iota(jnp.int32, (n, 2 * n), 0)
    kb = lax.broadcasted_iota(jnp.int32, (n, 2 * n), 1)
    cur = kb >= n
    lane = lax.broadcasted_iota(jnp.int32, (n, LANES), 1)
    left = lane < ATT_HEAD_DIM
    ones = jnp.ones((2 * n, LANES), bf16)

    for d in DILATIONS:
        shift = d.bit_length() - 1
        nch = dmax // d
        cr = n // nch
        cshift = cr.bit_length() - 1
        lq = (qa & (cr - 1)) * nch + (qa >> cshift)
        kbl = kb & (n - 1)
        kpos = (kb & n) + (kbl & (cr - 1)) * nch + (kbl >> cshift)
        rel = lq + n - kpos
        band = (rel >= 0) & (rel <= n)
        band_first = band & (cur | (tile > 0))
        group = ATT_GROUPS[d]

        def body(it, carry, d=d, shift=shift, nch=nch, cr=cr, band=band, band_first=band_first, group=group):
            def gather(ref, starts):
                parts = [ref[pl.ds(pl.multiple_of(st, SUBLANES), cr), :] for st in starts]
                return parts[0] if len(parts) == 1 else jnp.concatenate(parts, axis=0)

            loaded = []
            for g in range(group):
                idx = it * group + g
                m = idx >> shift
                r = idx & (d - 1)
                first = m == 0
                pbase = jnp.where(first, 0, ta) + jnp.where(first, nch - 1, m - 1) * cr
                q_st = [(d * c + r) * n + m * cr for c in range(nch)]
                k_st = [pbase + (d * c + r) * n for c in range(nch)] + [ta + st for st in q_st]
                q_r = gather(qbuf, q_st)
                k_r = gather(kbuf, k_st).astype(bf16)
                v_r = jnp.concatenate([gather(vbuf, k_st).astype(bf16), ones], axis=-1)
                old = None
                if d != DILATIONS[0]:
                    old = (gather(acc_o, q_st), gather(acc_m, q_st), gather(acc_l, q_st))
                valid = band_first if m == 0 else band
                loaded.append((q_st, q_r, k_r, v_r, old, valid))
            merged = []
            for q_st, q_r, k_r, v_r, old, valid in loaded:
                o_h, m_h, l_h = [], [], []
                for hh in range(2):
                    qm = jnp.where(left if hh == 0 else ~left, q_r, 0.0).astype(bf16)
                    s = jnp.where(valid, _dot_nt(qm, k_r), -1e30)
                    mx = jnp.max(s, axis=-1, keepdims=True)
                    pv = _dot(jnp.exp2(s - mx).astype(bf16), v_r)
                    m_h.append(mx)
                    o_h.append(pv[:, :LANES])
                    l_h.append(pv[:, LANES:])
                o_new = jnp.where(left, o_h[0], o_h[1])
                m_new = jnp.where(left, m_h[0], m_h[1])
                l_new = jnp.where(left, l_h[0], l_h[1])
                if old is not None:
                    o_old, m_old, l_old = old
                    m_tot = jnp.maximum(m_old, m_new)
                    a = jnp.exp2(m_old - m_tot)
                    b = jnp.exp2(m_new - m_tot)
                    o_new = a * o_old + b * o_new
                    l_new = a * l_old + b * l_new
                    m_new = m_tot
                merged.append((q_st, o_new, m_new, l_new))
            for q_st, o_new, m_new, l_new in merged:
                for c, st in enumerate(q_st):
                    dst = pl.ds(pl.multiple_of(st, SUBLANES), cr)
                    acc_o[dst, :] = o_new[c * cr:(c + 1) * cr, :]
                    acc_m[dst, :] = m_new[c * cr:(c + 1) * cr, :]
                    acc_l[dst, :] = l_new[c * cr:(c + 1) * cr, :]
            return carry

        for it in range(ta // n // group):
            body(it, 0)

    for r in range(dmax):
        o_ref[pl.ds(r, n, stride=dmax), :] = acc_o[r * n:(r + 1) * n, :] / acc_l[r * n:(r + 1) * n, :]
    kbuf[0:ta, :] = kbuf[ta:2 * ta, :]
    vbuf[0:ta, :] = vbuf[ta:2 * ta, :]


def _dilated_attention(proj, bsz, seq, q_norm, k_norm):
    t = bsz * seq
    ta = ATT_TILE
    nt = seq // ta
    pairs = ATT_HEADS // 2
    bsum = np.zeros((LANES, LANES), np.float32)
    for h in range(2):
        bsum[h * ATT_HEAD_DIM:(h + 1) * ATT_HEAD_DIM, h * ATT_HEAD_DIM:(h + 1) * ATT_HEAD_DIM] = 1.0 / ATT_HEAD_DIM
    blk = pl.BlockSpec((ta, LANES), lambda b, p, i: (b * nt + i, p))
    const = lambda b, p, i: (0, 0)
    vec = pl.BlockSpec((1, LANES), const)
    buf = lambda rows: pltpu.VMEM((rows, LANES), f32)
    return pl.pallas_call(
        _attn_kernel, grid=(bsz, pairs, nt),
        in_specs=[blk, blk, blk, vec, vec, pl.BlockSpec((LANES, LANES), const)],
        out_specs=blk,
        out_shape=jax.ShapeDtypeStruct((t, GROUP_WIDTH), f32),
        scratch_shapes=[buf(ta), buf(2 * ta), buf(2 * ta), buf(ta), buf(ta), buf(ta)],
        compiler_params=_params(3), name="dilated_attention",
    )(proj["q"], proj["k"], proj["v"], jnp.tile(q_norm, 2).reshape(1, LANES), jnp.tile(k_norm, 2).reshape(1, LANES),
      jnp.asarray(bsum, bf16))


def _memkv_kernel(mem_ref, gm_ref, wkv_ref, gk_ref, k_ref, v_ref):
    hm = _rms(mem_ref[...], gm_ref[...]).astype(bf16)
    kv = _dot(hm, wkv_ref[...])
    gk = gk_ref[...]
    for hh in range(CA_HEADS):
        sl = slice(hh * CA_HEAD_DIM, (hh + 1) * CA_HEAD_DIM)
        k_ref[:, sl] = _rms(kv[:, sl], gk).astype(bf16)
    v_ref[...] = kv[:, D_MODEL:].astype(bf16)


def _memory_kv(mem, norm_mem, wkv_b, k_norm):
    bsz, m, _ = mem.shape
    const = lambda b: (0, 0)
    blk = pl.BlockSpec((None, m, D_MODEL), lambda b: (b, 0, 0))
    return pl.pallas_call(
        _memkv_kernel, grid=(bsz,),
        in_specs=[blk, pl.BlockSpec((1, D_MODEL), const), pl.BlockSpec((D_MODEL, 2 * D_MODEL), const),
                  pl.BlockSpec((1, CA_HEAD_DIM), const)],
        out_specs=[blk, blk],
        out_shape=[jax.ShapeDtypeStruct((bsz, m, D_MODEL), bf16)] * 2,
        compiler_params=_params(1), name="memory_kv",
    )(mem, norm_mem.reshape(1, -1), wkv_b, k_norm.reshape(1, -1))


def _outca_kernel(x_ref, y0_ref, y1_ref, y2_ref, y3_ref, w0_ref, w1_ref, w2_ref, w3_ref, gca_ref, wq_ref, kn_ref,
                  v_ref, gq_ref, wo_ref, o_ref):
    x1 = x_ref[...]
    for y_ref, w_ref in ((y0_ref, w0_ref), (y1_ref, w1_ref), (y2_ref, w2_ref), (y3_ref, w3_ref)):
        x1 = x1 + _dot(y_ref[...].astype(bf16), w_ref[...])
    h = _rms(x1, gca_ref[...]).astype(bf16)
    q = _dot(h, wq_ref[...])
    gq = gq_ref[...]
    outs = []
    for hh in range(CA_HEADS):
        sl = slice(hh * CA_HEAD_DIM, (hh + 1) * CA_HEAD_DIM)
        qn = (_rms(q[:, sl], gq) * (CA_HEAD_DIM ** -0.5)).astype(bf16)
        s = _dot_nt(qn, kn_ref[:, sl])
        m = jnp.max(s, axis=-1, keepdims=True)
        p = jnp.exp(s - m)
        l = jnp.sum(p, axis=-1, keepdims=True)
        outs.append((_dot(p.astype(bf16), v_ref[:, sl]) / l).astype(bf16))
    o_ref[...] = x1 + _dot(jnp.concatenate(outs, axis=-1), wo_ref[...])


def _outproj_cross_attention(x2d, ys, w_out_b, bsz, seq, norm_ca, wq_b, kn, vv, q_norm, wo_b):
    t = bsz * seq
    tm = TM_CA
    ns = seq // tm
    m = kn.shape[1]
    row = lambda b, i: (b * ns + i, 0)
    const = lambda b, i: (0, 0)
    mix = pl.BlockSpec((tm, GROUP_WIDTH), row)
    wblk = pl.BlockSpec((GROUP_WIDTH, D_MODEL), const)
    sq = pl.BlockSpec((D_MODEL, D_MODEL), const)
    kvb = pl.BlockSpec((None, m, D_MODEL), lambda b, i: (b, 0, 0))
    w_parts = [w_out_b[j * GROUP_WIDTH:(j + 1) * GROUP_WIDTH] for j in range(4)]
    return pl.pallas_call(
        _outca_kernel, grid=(bsz, ns),
        in_specs=[pl.BlockSpec((tm, D_MODEL), row), mix, mix, mix, mix, wblk, wblk, wblk, wblk,
                  pl.BlockSpec((1, D_MODEL), const), sq, kvb, kvb, pl.BlockSpec((1, CA_HEAD_DIM), const), sq],
        out_specs=pl.BlockSpec((tm, D_MODEL), row),
        out_shape=jax.ShapeDtypeStruct((t, D_MODEL), f32),
        compiler_params=_params(2), name="outproj_cross_attention",
    )(x2d, *ys, *w_parts, norm_ca.reshape(1, -1), wq_b, kn, vv, q_norm.reshape(1, -1), wo_b)


def _to_token_tiles(ref, x):
    rows = x.shape[0]
    for s in range(TOKEN_TILE_ROWS):
        ref[pl.ds(s, rows, stride=TOKEN_TILE_ROWS), :] = x[:, s * LANES:(s + 1) * LANES]


def _from_token_tiles(ref, rows):
    return jnp.concatenate([ref[pl.ds(s, rows, stride=TOKEN_TILE_ROWS), :] for s in range(TOKEN_TILE_ROWS)], axis=-1)


def _router_kernel(x_ref, g_ref, w1_ref, w2_ref, b_ref, slt_ref, h_ref, meta_ref, g1_ref, g2_ref, cnt_ref, base):
    @pl.when(pl.program_id(0) == 0)
    def _():
        base[...] = jnp.zeros_like(base)

    tm = TM_ROUTER
    h = _rms(x_ref[...], g_ref[...])
    _to_token_tiles(h_ref, h)
    h1, h2 = _split2(h)
    w1 = w1_ref[...]
    logits = (_dot(h1, w1) + _dot(h1, w2_ref[...]) + _dot(h2, w1)) + b_ref[...]
    lane = lax.broadcasted_iota(jnp.int32, (tm, LANES), 1)
    lanef = lane.astype(f32)
    neg = -jnp.inf

    lgm = jnp.where((lane >= N_EXPERTS) & (lane < N_EXPERTS + N_EXPERT_GROUPS), logits, neg)
    mg = jnp.max(lgm, axis=-1, keepdims=True)
    g_p = 1.0 / jnp.sum(jnp.exp(lgm - mg), axis=-1, keepdims=True)
    gi = jnp.min(jnp.where(lgm == mg, lanef, float(LANES)), axis=-1, keepdims=True) - float(N_EXPERTS)

    grp = (lane >> (EXPERTS_PER_GROUP.bit_length() - 1)).astype(f32)
    lem = jnp.where((grp == gi) & (lane < N_EXPERTS), logits, neg)
    me = jnp.max(lem, axis=-1, keepdims=True)
    i1 = jnp.min(jnp.where(lem == me, lanef, float(LANES)), axis=-1, keepdims=True)
    le2 = jnp.where(lanef == i1, neg, lem)
    m2 = jnp.max(le2, axis=-1, keepdims=True)
    i2 = jnp.min(jnp.where(le2 == m2, lanef, float(LANES)), axis=-1, keepdims=True)
    e2 = jnp.exp(m2 - me)
    gate1 = g_p / (1.0 + e2)
    gate2 = g_p * e2 / (1.0 + e2)

    oh1 = (lanef == i1).astype(f32)
    oh2 = (lanef == i2).astype(f32)
    cnt = oh1 + oh2
    before = _dot(slt_ref[...], cnt.astype(bf16)) + base[...]
    r1 = jnp.sum(oh1 * before, axis=-1, keepdims=True)
    r2 = jnp.sum(oh2 * before, axis=-1, keepdims=True)
    total = base[...] + jnp.sum(cnt, axis=0, keepdims=True)
    base[...] = total
    cnt_ref[...] = total

    packed = jnp.where(lane == 0, i1, jnp.where(lane == 1, i2, jnp.where(lane == 2, r1, r2)))
    meta_ref[...] = packed.T[0:SUBLANES, :].astype(jnp.int32)
    g1_ref[...] = jnp.broadcast_to(gate1, (tm, LANES))
    g2_ref[...] = jnp.broadcast_to(gate2, (tm, LANES))


def _router(x2d, norm_g, wg, bg, we, be):
    t = x2d.shape[0]
    tm = TM_ROUTER
    pad = LANES - N_EXPERTS - N_EXPERT_GROUPS
    wr = jnp.pad(jnp.concatenate([we, wg], axis=1), ((0, 0), (0, pad)))
    br = jnp.pad(jnp.concatenate([be, bg]), (0, pad)).reshape(1, LANES)
    w1, w2 = _split2(wr)
    slt = jnp.asarray(np.tril(np.ones((tm, tm), np.float32), -1), bf16)
    const = lambda i: (0, 0)
    row = lambda i: (i, 0)
    wspec = pl.BlockSpec((D_MODEL, LANES), const)
    gate = pl.BlockSpec((tm, LANES), row)
    return pl.pallas_call(
        _router_kernel, grid=(t // tm,),
        in_specs=[pl.BlockSpec((tm, D_MODEL), row), pl.BlockSpec((1, D_MODEL), const), wspec, wspec,
                  pl.BlockSpec((1, LANES), const), pl.BlockSpec((tm, tm), const)],
        out_specs=[pl.BlockSpec((tm * TOKEN_TILE_ROWS, LANES), row), pl.BlockSpec((SUBLANES, tm), lambda i: (0, i)),
                   gate, gate, pl.BlockSpec((1, LANES), const)],
        out_shape=[jax.ShapeDtypeStruct((t * TOKEN_TILE_ROWS, LANES), f32), jax.ShapeDtypeStruct((SUBLANES, t), jnp.int32),
                   jax.ShapeDtypeStruct((t, LANES), f32), jax.ShapeDtypeStruct((t, LANES), f32),
                   jax.ShapeDtypeStruct((1, LANES), f32)],
        scratch_shapes=[pltpu.VMEM((1, LANES), f32)],
        compiler_params=_params(1), name="moe_router",
    )(x2d, norm_g.reshape(1, -1), w1, w2, br, slt)


def _tile_copy(src, src_tile, dst, dst_tile, sem):
    rows = TOKEN_TILE_ROWS
    return pltpu.make_async_copy(src.at[pl.ds(pl.multiple_of(src_tile * rows, rows), rows)],
                                 dst.at[pl.ds(pl.multiple_of(dst_tile * rows, rows), rows)], sem)


def _dispatch_kernel(starts_ref, e1_ref, e2_ref, r1_ref, r2_ref, h_ref, init_ref, buf_ref, sem):
    del init_ref

    def issue(j, carry):
        _tile_copy(h_ref, j, buf_ref, starts_ref[e1_ref[j]] + r1_ref[j], sem).start(priority=0)
        _tile_copy(h_ref, j, buf_ref, starts_ref[e2_ref[j]] + r2_ref[j], sem).start(priority=1)
        return carry

    lax.fori_loop(0, TD_DISPATCH, issue, 0, unroll=DMA_UNROLL)

    def drain(j, carry):
        _tile_copy(h_ref, 0, buf_ref, 0, sem).wait()
        _tile_copy(h_ref, 0, buf_ref, 0, sem).wait()
        return carry

    lax.fori_loop(0, TD_DISPATCH, drain, 0, unroll=DMA_UNROLL)


def _dispatch(h_tiles, meta, starts_pad, init_buf):
    t = meta.shape[1]
    td = TD_DISPATCH
    smem = pl.BlockSpec((td,), lambda i, s: (i,), memory_space=pltpu.SMEM)
    anyspec = pl.BlockSpec(memory_space=pl.ANY)
    hspec = pl.BlockSpec((td * TOKEN_TILE_ROWS, LANES), lambda i, s: (i, 0))
    grid_spec = pltpu.PrefetchScalarGridSpec(
        num_scalar_prefetch=1, grid=(t // td,), in_specs=[smem, smem, smem, smem, hspec, anyspec], out_specs=anyspec,
        scratch_shapes=[pltpu.SemaphoreType.DMA(())])
    return pl.pallas_call(
        _dispatch_kernel, grid_spec=grid_spec, out_shape=jax.ShapeDtypeStruct(init_buf.shape, f32),
        input_output_aliases={6: 0}, compiler_params=_params(1), name="moe_dispatch",
    )(starts_pad, meta[0], meta[1], meta[2], meta[3], h_tiles, init_buf)


def _expert_kernel(blk_e_ref, nused_ref, x_ref, wg_ref, wu_ref, wd_ref, o_ref):
    del blk_e_ref
    used = pl.program_id(0) < nused_ref[0]

    @pl.when(used)
    def _():
        x = _from_token_tiles(x_ref, MOE_BLOCK).astype(bf16)
        g = _dot(x, wg_ref[...].astype(bf16))
        u = _dot(x, wu_ref[...].astype(bf16))
        _to_token_tiles(o_ref, _dot((_silu(g) * u).astype(bf16), wd_ref[...].astype(bf16)))

    @pl.when(jnp.logical_not(used))
    def _():
        o_ref[...] = jnp.zeros_like(o_ref)


def _experts(buf, blk_e, n_used, layer, w_gate, w_up, w_down):
    n_blocks = buf.shape[0] // (MOE_BLOCK * TOKEN_TILE_ROWS)
    xspec = pl.BlockSpec((MOE_BLOCK * TOKEN_TILE_ROWS, LANES), lambda i, be, nu: (i, 0))
    wspec = lambda a, b: pl.BlockSpec((None, None, a, b), lambda i, be, nu: (layer, be[i], 0, 0))
    grid_spec = pltpu.PrefetchScalarGridSpec(
        num_scalar_prefetch=2, grid=(n_blocks,),
        in_specs=[xspec, wspec(D_MODEL, EXPERT_HIDDEN), wspec(D_MODEL, EXPERT_HIDDEN), wspec(EXPERT_HIDDEN, D_MODEL)],
        out_specs=xspec)
    return pl.pallas_call(
        _expert_kernel, grid_spec=grid_spec, out_shape=jax.ShapeDtypeStruct(buf.shape, f32),
        compiler_params=_params(1), name="moe_experts",
    )(blk_e, n_used, buf, w_gate, w_up, w_down)


def _combine_kernel(starts_ref, e1_ref, e2_ref, r1_ref, r2_ref, x_ref, g1_ref, g2_ref, yb_ref, o_ref, rows0, rows1, sem):
    tc = TC_COMBINE

    def issue(j, carry):
        _tile_copy(yb_ref, starts_ref[e1_ref[j]] + r1_ref[j], rows0, j, sem).start(priority=0)
        _tile_copy(yb_ref, starts_ref[e2_ref[j]] + r2_ref[j], rows1, j, sem).start(priority=1)
        return carry

    lax.fori_loop(0, tc, issue, 0, unroll=DMA_UNROLL)

    def drain(j, carry):
        _tile_copy(yb_ref, 0, rows0, 0, sem).wait()
        _tile_copy(yb_ref, 0, rows1, 0, sem).wait()
        return carry

    lax.fori_loop(0, tc, drain, 0, unroll=DMA_UNROLL)
    reps = D_MODEL // LANES
    g1 = jnp.concatenate([g1_ref[...]] * reps, axis=-1)
    g2 = jnp.concatenate([g2_ref[...]] * reps, axis=-1)
    o_ref[...] = x_ref[...] + (g1 * _from_token_tiles(rows0, tc) + g2 * _from_token_tiles(rows1, tc))


def _combine(x2d, g1, g2, yb, meta, starts_pad):
    t = x2d.shape[0]
    tc = TC_COMBINE
    smem = pl.BlockSpec((tc,), lambda i, s: (i,), memory_space=pltpu.SMEM)
    row = lambda i, s: (i, 0)
    tiles = pltpu.VMEM((tc * TOKEN_TILE_ROWS, LANES), f32)
    grid_spec = pltpu.PrefetchScalarGridSpec(
        num_scalar_prefetch=1, grid=(t // tc,),
        in_specs=[smem, smem, smem, smem, pl.BlockSpec((tc, D_MODEL), row), pl.BlockSpec((tc, LANES), row),
                  pl.BlockSpec((tc, LANES), row), pl.BlockSpec(memory_space=pl.ANY)],
        out_specs=pl.BlockSpec((tc, D_MODEL), row),
        scratch_shapes=[tiles, tiles, pltpu.SemaphoreType.DMA(())])
    return pl.pallas_call(
        _combine_kernel, grid_spec=grid_spec, out_shape=jax.ShapeDtypeStruct((t, D_MODEL), f32),
        compiler_params=_params(1), name="moe_combine",
    )(starts_pad, meta[0], meta[1], meta[2], meta[3], x2d, g1, g2, yb)


def _moe(x2d, norm_g, wg, bg, we, be, layer, w_gate, w_up, w_down, init_buf):
    t = x2d.shape[0]
    h_tiles, meta, g1, g2, cnt = _router(x2d, norm_g, wg, bg, we, be)
    counts = cnt[0, :N_EXPERTS].astype(jnp.int32)
    padded = ((counts + MOE_BLOCK - 1) // MOE_BLOCK) * MOE_BLOCK
    ends_pad = jnp.cumsum(padded)
    starts_pad = (ends_pad - padded).astype(jnp.int32)
    a = t * TOP_K
    p_rows = (-(-a // MOE_BLOCK)) * MOE_BLOCK + N_EXPERTS * MOE_BLOCK
    n_blocks = p_rows // MOE_BLOCK
    blk_start = jnp.arange(n_blocks, dtype=jnp.int32) * MOE_BLOCK
    blk_e = jnp.minimum(jnp.sum((ends_pad[None, :] <= blk_start[:, None]).astype(jnp.int32), axis=1), N_EXPERTS - 1)
    n_used = (ends_pad[-1:] // MOE_BLOCK).astype(jnp.int32)
    if init_buf is None:
        init_buf = jnp.zeros((p_rows * TOKEN_TILE_ROWS, LANES), f32)
    buf = _dispatch(h_tiles, meta, starts_pad, init_buf)
    yb = _experts(buf, blk_e, n_used, layer, w_gate, w_up, w_down)
    return _combine(x2d, g1, g2, yb, meta, starts_pad), buf


def _prep_w_in_kernel(w_ref, o_ref):
    cut = GROUP_WIDTH + SSD_CONV_CH
    o_ref[:, 0:cut] = w_ref[:, 0:cut].astype(bf16)
    lane = lax.broadcasted_iota(jnp.int32, (W_PREP_ROWS, LANES), 1)
    o_ref[:, cut:cut + LANES] = jnp.where(lane < SSD_HEADS, w_ref[:, cut:cut + LANES], 0.0).astype(bf16)
    tail = w_ref[:, cut:N_IN_RAW]
    o_ref[:, cut + LANES:N_PROJ] = tail[:, SSD_HEADS:SSD_HEADS + N_PROJ - cut - LANES].astype(bf16)


def _prep_w_in(w_in):
    depth = w_in.shape[0]
    rows = W_PREP_ROWS
    return pl.pallas_call(
        _prep_w_in_kernel, grid=(depth, D_MODEL // rows),
        in_specs=[pl.BlockSpec((None, rows, N_IN_RAW), lambda l, i: (l, i, 0))],
        out_specs=pl.BlockSpec((None, rows, N_PROJ), lambda l, i: (l, i, 0)),
        out_shape=jax.ShapeDtypeStruct((depth, D_MODEL, N_PROJ), bf16),
        compiler_params=_params(2), name="prep_w_in",
    )(w_in)


def kernel(x, mem, norm_mix, w_in, ssd_conv_w, ssd_conv_b, ssd_dt_bias, ssd_a_log, ssd_d, ssd_norm, sc_conv_w,
           attn_q_norm, attn_k_norm, cf_conv_w, cf_conv_b, cf_ln_g, cf_ln_b, w_out, norm_ca, norm_mem, ca_wq, ca_wkv,
           ca_q_norm, ca_k_norm, ca_wo, norm_ffn, router_group_w, router_group_b, router_expert_w, router_expert_b,
           exp_w_gate, exp_w_up, exp_w_down):
    bsz, seq, d = x.shape
    depth = w_in.shape[0]
    assert d == D_MODEL and seq % ATT_TILE == 0 and seq % TS_SSD == 0 and (bsz * seq) % TM_PROJ == 0
    x2d = x.reshape(bsz * seq, d)
    w_cat = _prep_w_in(w_in)
    moe_buf = None
    for l in range(depth):
        proj = _in_projection(x2d, norm_mix[l].reshape(1, -1), l, w_cat)
        y_ssd = _ssd(proj, bsz, seq, ssd_conv_w[l], ssd_conv_b[l], ssd_dt_bias[l], ssd_a_log[l], ssd_d[l], ssd_norm[l])
        y_sc, y_cf = _convs(proj, bsz, seq, sc_conv_w[l], cf_conv_w[l], cf_conv_b[l], cf_ln_g[l], cf_ln_b[l])
        y_att = _dilated_attention(proj, bsz, seq, attn_q_norm[l], attn_k_norm[l])
        kn, vv = _memory_kv(mem, norm_mem[l], ca_wkv[l].astype(bf16), ca_k_norm[l])
        x2d = _outproj_cross_attention(x2d, (y_ssd, y_sc, y_att, y_cf), w_out[l].astype(bf16), bsz, seq, norm_ca[l],
                                       ca_wq[l].astype(bf16), kn, vv, ca_q_norm[l], ca_wo[l].astype(bf16))
        x2d, moe_buf = _moe(x2d, norm_ffn[l], router_group_w[l], router_group_b[l], router_expert_w[l],
                            router_expert_b[l], l, exp_w_gate, exp_w_up, exp_w_down, moe_buf)
    return x2d.reshape(bsz, seq, d)
```
